```python
import math
import jax, jax.numpy as jnp
from jax import lax
import numpy as np

D_MODEL = 2048
BATCH = 4
SEQ = 2048
DEPTH = 1
DEC_BATCH = 128
DEC_SEQ = 1
PAST_LEN = 16384
PAGE_SIZE = 128

N_META = 16
MIX_WIDTH = D_MODEL
GDN_HEADS = 8
GDN_HEAD_K = 128
GDN_HEAD_V = 128
GDN_WIDTH = GDN_HEADS * GDN_HEAD_V
CONF_WIDTH = MIX_WIDTH - GDN_WIDTH
SHORT_CONV = 4
CONF_KERNEL = 31
GDN_CHUNK = 64
QKV_WIDTH = 2 * GDN_HEADS * GDN_HEAD_K + GDN_WIDTH
IN_COLS = QKV_WIDTH + 2 * GDN_HEADS + GDN_WIDTH + 3 * CONF_WIDTH
RMS_EPS = 1e-6
LN_EPS = 1e-5

kernel_name = "hymba_gdn_conformer_step"


def _rmsnorm(x, w):
    xf = x.astype(jnp.float32)
    y = xf * lax.rsqrt(jnp.mean(xf * xf, axis=-1, keepdims=True) + RMS_EPS)
    return (y * w.astype(jnp.float32)).astype(x.dtype)


def _l2norm(x):
    return x * lax.rsqrt(jnp.sum(x * x, axis=-1, keepdims=True) + RMS_EPS)


def _causal_dwconv(x, hist, w):
    xp = jnp.concatenate([hist.astype(x.dtype), x], axis=1)
    y = lax.conv_general_dilated(xp, w.astype(x.dtype)[:, None, :], (1,), 'VALID',
                                 dimension_numbers=('NWC', 'WIO', 'NWC'),
                                 feature_group_count=x.shape[-1])
    return y, xp[:, xp.shape[1] - (w.shape[0] - 1):]


def _gdn_chunk(S, inp):
    q, k, v, g, beta = inp
    C = q.shape[-2]
    gcum = jnp.cumsum(g, axis=-1)
    idx = jnp.arange(C)
    incl = idx[:, None] >= idx[None, :]
    strict = idx[:, None] > idx[None, :]
    diff = gcum[..., :, None] - gcum[..., None, :]
    gam = jnp.where(incl, jnp.exp(jnp.where(incl, diff, 0.0)), 0.0)
    a = jnp.where(strict, beta[..., :, None] * jnp.einsum('bhik,bhjk->bhij', k, k) * gam, 0.0)
    eye = jnp.eye(C, dtype=jnp.float32)
    rhs = jnp.concatenate([beta[..., None] * v, (beta * jnp.exp(gcum))[..., None] * k], axis=-1)
    sol = lax.linalg.triangular_solve(a + eye, rhs, left_side=True, lower=True, unit_diagonal=True)
    u, wk = sol[..., :GDN_HEAD_V], sol[..., GDN_HEAD_V:]
    v_new = u - jnp.einsum('bhck,bhkv->bhcv', wk, S)
    o = (jnp.exp(gcum)[..., None] * jnp.einsum('bhck,bhkv->bhcv', q, S)
         + jnp.einsum('bhij,bhjv->bhiv', jnp.einsum('bhik,bhjk->bhij', q, k) * gam, v_new))
    g_last = gcum[..., -1:]
    S_new = (jnp.exp(g_last)[..., None] * S
             + jnp.einsum('bhck,bhcv->bhkv', k * jnp.exp(g_last - gcum)[..., None], v_new))
    return S_new, o


def _gdn_scan(q, k, v, g, beta, S0):
    B, L = q.shape[0], q.shape[1]
    C = GDN_CHUNK if L % GDN_CHUNK == 0 else L
    n = L // C
    to_chunks = lambda t: jnp.moveaxis(t.reshape((B, n, C) + t.shape[2:]), (1, 3), (0, 2))
    S, o = lax.scan(_gdn_chunk, S0, (to_chunks(q), to_chunks(k), to_chunks(v), to_chunks(g), to_chunks(beta)))
    o = jnp.moveaxis(o, (0, 2), (1, 3)).reshape(B, L, GDN_HEADS, GDN_HEAD_V)
    return o, S


def _mixer(h, S0, gconv_hist, cconv_hist, n_lead, w_in, gdn_conv_w, A_log, dt_bias, gdn_norm_w,
           conf_dw_w, conf_dw_b, conf_ln_w, conf_ln_b, w_out):
    B, L, _ = h.shape
    p = jnp.einsum('bld,dc->blc', h, w_in)
    o0 = QKV_WIDTH
    o1 = o0 + GDN_HEADS
    o2 = o1 + GDN_HEADS
    o3 = o2 + GDN_WIDTH
    o4 = o3 + CONF_WIDTH
    o5 = o4 + CONF_WIDTH
    f32 = jnp.float32
    qkv, gconv_new = _causal_dwconv(p[..., :o0], gconv_hist, gdn_conv_w)
    qkv = jax.nn.silu(qkv).astype(f32)
    nk = GDN_HEADS * GDN_HEAD_K
    q = _l2norm(qkv[..., :nk].reshape(B, L, GDN_HEADS, GDN_HEAD_K)) * (GDN_HEAD_K ** -0.5)
    k = _l2norm(qkv[..., nk:2 * nk].reshape(B, L, GDN_HEADS, GDN_HEAD_K))
    v = qkv[..., 2 * nk:].reshape(B, L, GDN_HEADS, GDN_HEAD_V)
    g = -jnp.exp(A_log.astype(f32)) * jax.nn.softplus(p[..., o0:o1].astype(f32) + dt_bias.astype(f32))
    beta = jax.nn.sigmoid(p[..., o1:o2].astype(f32))
    S = S0.astype(f32)
    if n_lead > 0:
        o_a, S = _gdn_scan(q[:, :n_lead], k[:, :n_lead], v[:, :n_lead], g[:, :n_lead], beta[:, :n_lead], S)
        o_b, S = _gdn_scan(q[:, n_lead:], k[:, n_lead:], v[:, n_lead:], g[:, n_lead:], beta[:, n_lead:], S)
        o = jnp.concatenate([o_a, o_b], axis=1)
    else:
        o, S = _gdn_scan(q, k, v, g, beta, S)
    o = _rmsnorm(o, gdn_norm_w).reshape(B, L, GDN_WIDTH)
    gdn_out = (o * jax.nn.silu(p[..., o2:o3].astype(f32))).astype(h.dtype)
    glu = p[..., o3:o4] * jax.nn.sigmoid(p[..., o4:o5])
    c, cconv_new = _causal_dwconv(glu, cconv_hist, conf_dw_w)
    c = (c + conf_dw_b.astype(c.dtype)).astype(f32)
    mu = jnp.mean(c, axis=-1, keepdims=True)
    cc = c - mu
    c = cc * lax.rsqrt(jnp.mean(cc * cc, axis=-1, keepdims=True) + LN_EPS)
    c = c * conf_ln_w.astype(f32) + conf_ln_b.astype(f32)
    conf_out = (jax.nn.silu(c) * jax.nn.silu(p[..., o5:].astype(f32))).astype(h.dtype)
    y = jnp.einsum('blc,cd->bld', jnp.concatenate([gdn_out, conf_out], axis=-1), w_out)
    return y, S, gconv_new, cconv_new


def setup_inputs(seed: int = 0) -> dict:
    key = jax.random.key(seed)
    ks = jax.random.split(key, 20)
    f32 = jnp.float32
    nrm = lambda k, s, sc: jax.random.normal(k, s, f32) * sc
    dt = jnp.exp(jax.random.uniform(ks[9], (DEPTH, GDN_HEADS), f32, math.log(0.001), math.log(0.1)))
    return {
        "x_prompt": nrm(ks[0], (BATCH, SEQ, D_MODEL), 1.0),
        "x_sample": nrm(ks[1], (DEC_BATCH, DEC_SEQ, D_MODEL), 1.0),
        "state_gdn_S": nrm(ks[2], (DEPTH, DEC_BATCH, GDN_HEADS, GDN_HEAD_K, GDN_HEAD_V), GDN_HEAD_K ** -0.5),
        "state_gdn_conv": nrm(ks[3], (DEPTH, DEC_BATCH, SHORT_CONV - 1, QKV_WIDTH), 1.0),
        "state_conf_conv": nrm(ks[4], (DEPTH, DEC_BATCH, CONF_KERNEL - 1, CONF_WIDTH), 0.5),
        "meta_tokens": nrm(ks[5], (N_META, D_MODEL), 1.0),
        "norm_w": 1.0 + nrm(ks[6], (DEPTH, D_MODEL), 0.02),
        "w_in": nrm(ks[7], (DEPTH, D_MODEL, IN_COLS), D_MODEL ** -0.5),
        "gdn_conv_w": nrm(ks[8], (DEPTH, SHORT_CONV, QKV_WIDTH), SHORT_CONV ** -0.5),
        "gdn_A_log": jnp.log(jax.random.uniform(ks[10], (DEPTH, GDN_HEADS), f32, 1.0, 16.0)),
        "gdn_dt_bias": dt + jnp.log(-jnp.expm1(-dt)),
        "gdn_norm_w": 1.0 + nrm(ks[11], (DEPTH, GDN_HEAD_V), 0.02),
        "conf_dw_w": nrm(ks[12], (DEPTH, CONF_KERNEL, CONF_WIDTH), CONF_KERNEL ** -0.5),
        "conf_dw_b": nrm(ks[13], (DEPTH, CONF_WIDTH), 0.02),
        "conf_ln_w": 1.0 + nrm(ks[14], (DEPTH, CONF_WIDTH), 0.02),
        "conf_ln_b": nrm(ks[15], (DEPTH, CONF_WIDTH), 0.02),
        "w_out": nrm(ks[16], (DEPTH, MIX_WIDTH, D_MODEL), MIX_WIDTH ** -0.5),
        "final_norm_w": 1.0 + nrm(ks[17], (D_MODEL,), 0.02),
    }


def reference(x_prompt, x_sample, state_gdn_S, state_gdn_conv, state_conf_conv, meta_tokens,
              norm_w, w_in, gdn_conv_w, gdn_A_log, gdn_dt_bias, gdn_norm_w,
              conf_dw_w, conf_dw_b, conf_ln_w, conf_ln_b, w_out, final_norm_w):
    B = x_prompt.shape[0]
    meta = jnp.broadcast_to(meta_tokens.astype(x_prompt.dtype)[None], (B, N_META, D_MODEL))
    hp = jnp.concatenate([meta, x_prompt], axis=1)
    hs = x_sample
    Sp_l, gp_l, cp_l, Ss_l, gs_l, cs_l = [], [], [], [], [], []
    for l in range(DEPTH):
        w = (w_in[l], gdn_conv_w[l], gdn_A_log[l], gdn_dt_bias[l], gdn_norm_w[l],
             conf_dw_w[l], conf_dw_b[l], conf_ln_w[l], conf_ln_b[l], w_out[l])
        S0p = jnp.zeros((B, GDN_HEADS, GDN_HEAD_K, GDN_HEAD_V), jnp.float32)
        g0p = jnp.zeros((B, SHORT_CONV - 1, QKV_WIDTH), hp.dtype)
        c0p = jnp.zeros((B, CONF_KERNEL - 1, CONF_WIDTH), hp.dtype)
        yp, Sp, gp, cp = _mixer(_rmsnorm(hp, norm_w[l]), S0p, g0p, c0p, N_META, *w)
        hp = hp + yp
        ys, Ss, gs, cs = _mixer(_rmsnorm(hs, norm_w[l]), state_gdn_S[l], state_gdn_conv[l],
                                state_conf_conv[l], 0, *w)
        hs = hs + ys
        Sp_l.append(Sp.astype(state_gdn_S.dtype))
        gp_l.append(gp.astype(state_gdn_conv.dtype))
        cp_l.append(cp.astype(state_conf_conv.dtype))
        Ss_l.append(Ss.astype(state_gdn_S.dtype))
        gs_l.append(gs.astype(state_gdn_conv.dtype))
        cs_l.append(cs.astype(state_conf_conv.dtype))
    y_prompt = _rmsnorm(hp[:, N_META:], final_norm_w)
    y_sample = _rmsnorm(hs, final_norm_w)
    return (y_prompt, y_sample, jnp.stack(Sp_l), jnp.stack(gp_l), jnp.stack(cp_l),
            jnp.stack(Ss_l), jnp.stack(gs_l), jnp.stack(cs_l))
```

```python
import functools

import jax
import jax.numpy as jnp
from jax import lax
from jax.experimental import pallas as pl
from jax.experimental.pallas import tpu as pltpu

F32 = jnp.float32
BF16 = jnp.bfloat16

D_MODEL = 2048
N_META = 16
HEADS = 8
HEAD_K = 128
HEAD_V = 128
GDN_WIDTH = HEADS * HEAD_V
CONF_WIDTH = 1024
QKV_WIDTH = 3 * GDN_WIDTH
SHORT_CONV = 4
CONF_KERNEL = 31
GDN_CHUNK = 64
RMS_EPS = 1e-6
LN_EPS = 1e-5

C_QKV = 0
C_ZG = QKV_WIDTH
C_GA = C_ZG + GDN_WIDTH
C_GB = C_GA + CONF_WIDTH
C_ZC = C_GB + CONF_WIDTH
C_AB = C_ZC + CONF_WIDTH
LANES = 128
SUBLANES = 8
P_COLS = C_AB + LANES

GHIST_ROWS = SUBLANES
CHIST_ROWS = 32

VMEM_LIMIT = 56 * 1024 * 1024

INV_PASSES = 3


def _silu(x):
    return x * jax.nn.sigmoid(x)


def _softplus(x):
    return jnp.maximum(x, 0.0) + jnp.log1p(jnp.exp(-jnp.abs(x)))


def _split_bf16(x):
    hi = x.astype(BF16)
    lo = (x - hi.astype(F32)).astype(BF16)
    return hi, lo


def _mm(a, b, passes=1):
    if passes == 1:
        return jnp.dot(a.astype(BF16), b.astype(BF16), preferred_element_type=F32)
    a_hi, a_lo = _split_bf16(a)
    b_hi, b_lo = _split_bf16(b)
    return (jnp.dot(a_hi, b_hi, preferred_element_type=F32)
            + (jnp.dot(a_hi, b_lo, preferred_element_type=F32)
               + jnp.dot(a_lo, b_hi, preferred_element_type=F32)))


def _mm_nt(a, b):
    return lax.dot_general(a.astype(BF16), b.astype(BF16), (((1,), (1,)), ((), ())),
                           preferred_element_type=F32)


def _mm_tn(a, b):
    return lax.dot_general(a.astype(BF16), b.astype(BF16), (((0,), (0,)), ((), ())),
                           preferred_element_type=F32)


def _norm_cast_kernel(x_ref, w_ref, o_ref):
    x = x_ref[...]
    y = x * lax.rsqrt(jnp.mean(x * x, axis=-1, keepdims=True) + RMS_EPS)
    o_ref[...] = (y * w_ref[...]).astype(BF16)


def _norm_cast(x, w, tm):
    m = x.shape[0]
    return pl.pallas_call(
        _norm_cast_kernel,
        grid=(m // tm,),
        in_specs=[pl.BlockSpec((tm, D_MODEL), lambda i: (i, 0)),
                  pl.BlockSpec((1, D_MODEL), lambda i: (0, 0))],
        out_specs=pl.BlockSpec((tm, D_MODEL), lambda i: (i, 0)),
        out_shape=jax.ShapeDtypeStruct((m, D_MODEL), BF16),
        compiler_params=pltpu.CompilerParams(dimension_semantics=("parallel",),
                                             vmem_limit_bytes=VMEM_LIMIT),
        name="norm_cast",
    )(x, w)


def _matmul_kernel(x_ref, w_ref, o_ref):
    o_ref[...] = jnp.dot(x_ref[...], w_ref[...], preferred_element_type=F32)


def _matmul(x, w, tm, tn):
    m, k = x.shape
    n = w.shape[1]
    return pl.pallas_call(
        _matmul_kernel,
        grid=(n // tn, m // tm),
        in_specs=[pl.BlockSpec((tm, k), lambda j, i: (i, 0)),
                  pl.BlockSpec((k, tn), lambda j, i: (0, j))],
        out_specs=pl.BlockSpec((tm, tn), lambda j, i: (i, j)),
        out_shape=jax.ShapeDtypeStruct((m, n), F32),
        compiler_params=pltpu.CompilerParams(dimension_semantics=("parallel", "parallel"),
                                             vmem_limit_bytes=VMEM_LIMIT),
        name="in_proj",
    )(x, w)


def _tri_inverse(a, c, eye, blk_masks, passes):
    d0 = jnp.where(blk_masks[0], a, 0.0)
    x = eye - d0
    d2 = _mm(d0, d0, passes)
    x = x + _mm(x, d2, passes)
    d4 = _mm(d2, d2, passes)
    x = x + _mm(x, d4, passes)
    for lvl in range(1, len(blk_masks)):
        off = jnp.where(blk_masks[lvl] & jnp.logical_not(blk_masks[lvl - 1]), a, 0.0)
        x = x - _mm(_mm(x, off, passes), x, passes)
    return x


def _seq_kernel(p_ref, s0_ref, g0_ref, c0_ref, cw_ref, alog_ref, dtb_ref, gnw_ref,
                dww_ref, dwb_ref, lnw_ref, lnb_ref,
                gdn_ref, conf_ref, s_out_ref, g_out_ref, c_out_ref,
                s_scr, xbuf, cbuf, *, t, c):
    step = pl.program_id(1)
    last = pl.num_programs(1) - 1

    @pl.when(step == 0)
    def _():
        s_scr[...] = s0_ref[...]
        xbuf[0:GHIST_ROWS, :] = g0_ref[...]
        cbuf[0:CHIST_ROWS, :] = c0_ref[...]

    xbuf[GHIST_ROWS:GHIST_ROWS + t, :] = p_ref[0, :, C_QKV:C_QKV + QKV_WIDTH]
    base = GHIST_ROWS - (SHORT_CONV - 1)
    conv = cw_ref[0:1, :] * xbuf[base:base + t, :]
    for k in range(1, SHORT_CONV):
        conv = conv + cw_ref[k:k + 1, :] * xbuf[base + k:base + k + t, :]
    qkv = _silu(conv)
    new_hist = xbuf[t:t + GHIST_ROWS, :]
    xbuf[0:GHIST_ROWS, :] = new_hist

    ab = p_ref[0, :, C_AB:C_AB + LANES]
    g_all = -jnp.exp(alog_ref[...]) * _softplus(ab + dtb_ref[...])
    beta_all = jax.nn.sigmoid(ab)

    r = lax.broadcasted_iota(jnp.int32, (c, c), 0)
    q = lax.broadcasted_iota(jnp.int32, (c, c), 1)
    incl = r >= q
    strict = r > q
    eye = (r == q).astype(F32)
    tril = incl.astype(BF16)
    blk_masks = []
    b = 8
    while b <= c:
        sh = b.bit_length() - 1
        blk_masks.append((r >> sh) == (q >> sh))
        b *= 2
    gnw = gnw_ref[...]

    for ci in range(t // c):
        rows = slice(ci * c, (ci + 1) * c)
        g = g_all[rows]
        g_hi = g.astype(BF16)
        r1 = g - g_hi.astype(F32)
        g_mid = r1.astype(BF16)
        g_lo = (r1 - g_mid.astype(F32)).astype(BF16)
        gcum = (jnp.dot(tril, g_hi, preferred_element_type=F32)
                + (jnp.dot(tril, g_mid, preferred_element_type=F32)
                   + jnp.dot(tril, g_lo, preferred_element_type=F32)))
        gcum_t = gcum.T
        beta = beta_all[rows]
        for h in range(HEADS):
            qh = qkv[rows, h * HEAD_K:(h + 1) * HEAD_K]
            kh = qkv[rows, GDN_WIDTH + h * HEAD_K:GDN_WIDTH + (h + 1) * HEAD_K]
            vh = qkv[rows, 2 * GDN_WIDTH + h * HEAD_V:2 * GDN_WIDTH + (h + 1) * HEAD_V]
            qn = qh * lax.rsqrt(jnp.sum(qh * qh, axis=-1, keepdims=True) + RMS_EPS) * (HEAD_K ** -0.5)
            kn = kh * lax.rsqrt(jnp.sum(kh * kh, axis=-1, keepdims=True) + RMS_EPS)
            gc_col = gcum[:, h:h + 1]
            gc_row = gcum_t[h:h + 1, :]
            beta_col = beta[:, HEADS + h:HEADS + h + 1]
            gam = jnp.where(incl, jnp.exp(jnp.where(incl, gc_col - gc_row, 0.0)), 0.0)
            a = jnp.where(strict, beta_col * _mm_nt(kn, kn) * gam, 0.0)
            tinv = _tri_inverse(a, c, eye, blk_masks, INV_PASSES)
            eg = jnp.exp(gc_col)
            rhs = jnp.concatenate([beta_col * vh, (beta_col * eg) * kn], axis=-1)
            sol = _mm(tinv, rhs, INV_PASSES)
            u = sol[:, :HEAD_V]
            wk = sol[:, HEAD_V:]
            s = s_scr[h]
            v_new = u - _mm(wk, s)
            o = eg * _mm(qn, s) + _mm(_mm_nt(qn, kn) * gam, v_new)
            g_last = gc_col[c - 1:c, :]
            s_scr[h] = jnp.exp(g_last) * s + _mm_tn(kn * jnp.exp(g_last - gc_col), v_new)
            on = o * lax.rsqrt(jnp.mean(o * o, axis=-1, keepdims=True) + RMS_EPS) * gnw
            z = p_ref[0, rows, C_ZG + h * HEAD_V:C_ZG + (h + 1) * HEAD_V]
            gdn_ref[0, rows, h * HEAD_V:(h + 1) * HEAD_V] = (on * _silu(z)).astype(gdn_ref.dtype)

    glu = p_ref[0, :, C_GA:C_GA + CONF_WIDTH] * jax.nn.sigmoid(p_ref[0, :, C_GB:C_GB + CONF_WIDTH])
    cbuf[CHIST_ROWS:CHIST_ROWS + t, :] = glu
    cbase = CHIST_ROWS - (CONF_KERNEL - 1)
    acc = dww_ref[0:1, :] * cbuf[cbase:cbase + t, :]
    for k in range(1, CONF_KERNEL):
        acc = acc + dww_ref[k:k + 1, :] * cbuf[cbase + k:cbase + k + t, :]
    cv = acc + dwb_ref[...]
    mu = jnp.mean(cv, axis=-1, keepdims=True)
    cc = cv - mu
    cn = cc * lax.rsqrt(jnp.mean(cc * cc, axis=-1, keepdims=True) + LN_EPS)
    cn = cn * lnw_ref[...] + lnb_ref[...]
    zc = p_ref[0, :, C_ZC:C_ZC + CONF_WIDTH]
    conf_ref[0] = (_silu(cn) * _silu(zc)).astype(conf_ref.dtype)
    new_chist = cbuf[t:t + CHIST_ROWS, :]
    cbuf[0:CHIST_ROWS, :] = new_chist

    @pl.when(step == last)
    def _():
        s_out_ref[0] = s_scr[...]
        g_out_ref[0] = xbuf[0:GHIST_ROWS, :]
        c_out_ref[0] = cbuf[0:CHIST_ROWS, :]


def _seq_call(p, s0, g0, c0, wts, t, c):
    bsz, seq, _ = p.shape
    cw, alog, dtb, gnw, dww, dwb, lnw, lnb = wts
    full = lambda shape: pl.BlockSpec(shape, lambda b, s: (0,) * len(shape))
    return pl.pallas_call(
        functools.partial(_seq_kernel, t=t, c=c),
        grid=(bsz, seq // t),
        in_specs=[pl.BlockSpec((1, t, P_COLS), lambda b, s: (b, s, 0)),
                  full((HEADS, HEAD_K, HEAD_V)),
                  full((GHIST_ROWS, QKV_WIDTH)),
                  full((CHIST_ROWS, CONF_WIDTH)),
                  full(cw.shape), full(alog.shape), full(dtb.shape), full(gnw.shape),
                  full(dww.shape), full(dwb.shape), full(lnw.shape), full(lnb.shape)],
        out_specs=[pl.BlockSpec((1, t, GDN_WIDTH), lambda b, s: (b, s, 0)),
                   pl.BlockSpec((1, t, CONF_WIDTH), lambda b, s: (b, s, 0)),
                   pl.BlockSpec((1, HEADS, HEAD_K, HEAD_V), lambda b, s: (b, 0, 0, 0)),
                   pl.BlockSpec((1, GHIST_ROWS, QKV_WIDTH), lambda b, s: (b, 0, 0)),
                   pl.BlockSpec((1, CHIST_ROWS, CONF_WIDTH), lambda b, s: (b, 0, 0))],
        out_shape=[jax.ShapeDtypeStruct((bsz, seq, GDN_WIDTH), BF16),
                   jax.ShapeDtypeStruct((bsz, seq, CONF_WIDTH), BF16),
                   jax.ShapeDtypeStruct((bsz, HEADS, HEAD_K, HEAD_V), F32),
                   jax.ShapeDtypeStruct((bsz, GHIST_ROWS, QKV_WIDTH), F32),
                   jax.ShapeDtypeStruct((bsz, CHIST_ROWS, CONF_WIDTH), F32)],
        scratch_shapes=[pltpu.VMEM((HEADS, HEAD_K, HEAD_V), F32),
                        pltpu.VMEM((t + GHIST_ROWS, QKV_WIDTH), F32),
                        pltpu.VMEM((t + CHIST_ROWS, CONF_WIDTH), F32)],
        compiler_params=pltpu.CompilerParams(dimension_semantics=("parallel", "arbitrary"),
                                             vmem_limit_bytes=VMEM_LIMIT),
        name=f"seq_t{t}",
    )(p, s0, g0, c0, cw, alog, dtb, gnw, dww, dwb, lnw, lnb)


def _step_kernel(p_ref, s_ref, gh_ref, ch_ref, cw_ref, alog_ref, dtb_ref, gnw_ref,
                 dww_ref, dwb_ref, lnw_ref, lnb_ref,
                 gdn_ref, conf_ref, s_out_ref, g_out_ref, c_out_ref,
                 conv_scr, o_scr, cv_scr, *, nb):
    nh = SHORT_CONV - 1
    nc = CONF_KERNEL - 1
    for b in range(nb):
        x = p_ref[b:b + 1, C_QKV:C_QKV + QKV_WIDTH]
        hist = gh_ref[b]
        conv_scr[b:b + 1, :] = (jnp.sum(cw_ref[0:nh, :] * hist, axis=0, keepdims=True)
                                + cw_ref[nh:nh + 1, :] * x)
        g_out_ref[b, 0:nh - 1, :] = hist[1:nh]
        g_out_ref[b, nh - 1:nh, :] = x
        glu = (p_ref[b:b + 1, C_GA:C_GA + CONF_WIDTH]
               * jax.nn.sigmoid(p_ref[b:b + 1, C_GB:C_GB + CONF_WIDTH]))
        chist = ch_ref[b]
        cv_scr[b:b + 1, :] = (jnp.sum(dww_ref[0:nc, :] * chist, axis=0, keepdims=True)
                              + dww_ref[nc:nc + 1, :] * glu)
        c_out_ref[b, 0:nc - 1, :] = chist[1:nc]
        c_out_ref[b, nc - 1:nc, :] = glu

    qkv = _silu(conv_scr[...])
    ab = p_ref[:, C_AB:C_AB + LANES]
    g_all = -jnp.exp(alog_ref[...]) * _softplus(ab + dtb_ref[...])
    eg_all = jnp.exp(g_all)
    beta_all = jax.nn.sigmoid(ab)
    pad = jnp.zeros((HEAD_K - nb, HEAD_K), F32)

    for h in range(HEADS):
        qh = qkv[:, h * HEAD_K:(h + 1) * HEAD_K]
        kh = qkv[:, GDN_WIDTH + h * HEAD_K:GDN_WIDTH + (h + 1) * HEAD_K]
        vh = qkv[:, 2 * GDN_WIDTH + h * HEAD_V:2 * GDN_WIDTH + (h + 1) * HEAD_V]
        qn = qh * lax.rsqrt(jnp.sum(qh * qh, axis=-1, keepdims=True) + RMS_EPS) * (HEAD_K ** -0.5)
        kn = kh * lax.rsqrt(jnp.sum(kh * kh, axis=-1, keepdims=True) + RMS_EPS)
        qk = jnp.sum(qn * kn, axis=-1, keepdims=True)
        kt = jnp.concatenate([kn, pad], axis=0).T
        qt = jnp.concatenate([qn, pad], axis=0).T
        for b in range(nb):
            s = s_ref[b, h]
            kcol = kt[:, b:b + 1]
            qcol = qt[:, b:b + 1]
            ks = jnp.sum(kcol * s, axis=0, keepdims=True)
            qs = jnp.sum(qcol * s, axis=0, keepdims=True)
            eg = eg_all[b:b + 1, h:h + 1]
            beta = beta_all[b:b + 1, HEADS + h:HEADS + h + 1]
            v_new = beta * vh[b:b + 1, :] - (beta * eg) * ks
            o_scr[b:b + 1, h * HEAD_V:(h + 1) * HEAD_V] = eg * qs + qk[b:b + 1, :] * v_new
            s_out_ref[b, h] = eg * s + kcol * v_new

    gnw = gnw_ref[...]
    for h in range(HEADS):
        o = o_scr[:, h * HEAD_V:(h + 1) * HEAD_V]
        on = o * lax.rsqrt(jnp.mean(o * o, axis=-1, keepdims=True) + RMS_EPS) * gnw
        z = p_ref[:, C_ZG + h * HEAD_V:C_ZG + (h + 1) * HEAD_V]
        gdn_ref[:, h * HEAD_V:(h + 1) * HEAD_V] = (on * _silu(z)).astype(gdn_ref.dtype)

    cv = cv_scr[...] + dwb_ref[...]
    mu = jnp.mean(cv, axis=-1, keepdims=True)
    cc = cv - mu
    cn = cc * lax.rsqrt(jnp.mean(cc * cc, axis=-1, keepdims=True) + LN_EPS)
    cn = cn * lnw_ref[...] + lnb_ref[...]
    conf_ref[...] = (_silu(cn) * _silu(p_ref[:, C_ZC:C_ZC + CONF_WIDTH])).astype(conf_ref.dtype)


def _step_call(p, s, gh, ch, wts, nb, out_dtype):
    bsz = p.shape[0]
    cw, alog, dtb, gnw, dww, dwb, lnw, lnb = wts
    full = lambda shape: pl.BlockSpec(shape, lambda i: (0,) * len(shape))
    nh = SHORT_CONV - 1
    nc = CONF_KERNEL - 1
    return pl.pallas_call(
        functools.partial(_step_kernel, nb=nb),
        grid=(bsz // nb,),
        in_specs=[pl.BlockSpec((nb, P_COLS), lambda i: (i, 0)),
                  pl.BlockSpec((nb, HEADS, HEAD_K, HEAD_V), lambda i: (i, 0, 0, 0)),
                  pl.BlockSpec((nb, nh, QKV_WIDTH), lambda i: (i, 0, 0)),
                  pl.BlockSpec((nb, nc, CONF_WIDTH), lambda i: (i, 0, 0)),
                  full(cw.shape), full(alog.shape), full(dtb.shape), full(gnw.shape),
                  full(dww.shape), full(dwb.shape), full(lnw.shape), full(lnb.shape)],
        out_specs=[pl.BlockSpec((nb, GDN_WIDTH), lambda i: (i, 0)),
                   pl.BlockSpec((nb, CONF_WIDTH), lambda i: (i, 0)),
                   pl.BlockSpec((nb, HEADS, HEAD_K, HEAD_V), lambda i: (i, 0, 0, 0)),
                   pl.BlockSpec((nb, nh, QKV_WIDTH), lambda i: (i, 0, 0)),
                   pl.BlockSpec((nb, nc, CONF_WIDTH), lambda i: (i, 0, 0))],
        out_shape=[jax.ShapeDtypeStruct((bsz, GDN_WIDTH), BF16),
                   jax.ShapeDtypeStruct((bsz, CONF_WIDTH), BF16),
                   jax.ShapeDtypeStruct(s.shape, out_dtype),
                   jax.ShapeDtypeStruct(gh.shape, out_dtype),
                   jax.ShapeDtypeStruct(ch.shape, out_dtype)],
        scratch_shapes=[pltpu.VMEM((nb, QKV_WIDTH), F32),
                        pltpu.VMEM((nb, GDN_WIDTH), F32),
                        pltpu.VMEM((nb, CONF_WIDTH), F32)],
        compiler_params=pltpu.CompilerParams(dimension_semantics=("parallel",),
                                             vmem_limit_bytes=VMEM_LIMIT),
        name="decode_step",
    )(p, s, gh, ch, cw, alog, dtb, gnw, dww, dwb, lnw, lnb)


def _out_kernel(gdn_ref, conf_ref, w_ref, x_ref, fw_ref, o_ref):
    y = (jnp.dot(gdn_ref[...], w_ref[0:GDN_WIDTH, :], preferred_element_type=F32)
         + jnp.dot(conf_ref[...], w_ref[GDN_WIDTH:, :], preferred_element_type=F32))
    hres = x_ref[...] + y
    o_ref[...] = hres * lax.rsqrt(jnp.mean(hres * hres, axis=-1, keepdims=True) + RMS_EPS) * fw_ref[...]


def _out_call(gdn, conf, w, x, fw, tm):
    m = x.shape[0]
    return pl.pallas_call(
        _out_kernel,
        grid=(m // tm,),
        in_specs=[pl.BlockSpec((tm, GDN_WIDTH), lambda i: (i, 0)),
                  pl.BlockSpec((tm, CONF_WIDTH), lambda i: (i, 0)),
                  pl.BlockSpec(w.shape, lambda i: (0, 0)),
                  pl.BlockSpec((tm, D_MODEL), lambda i: (i, 0)),
                  pl.BlockSpec((1, D_MODEL), lambda i: (0, 0))],
        out_specs=pl.BlockSpec((tm, D_MODEL), lambda i: (i, 0)),
        out_shape=jax.ShapeDtypeStruct((m, D_MODEL), F32),
        compiler_params=pltpu.CompilerParams(dimension_semantics=("parallel",),
                                             vmem_limit_bytes=VMEM_LIMIT),
        name="out_proj",
    )(gdn, conf, w, x, fw)


def _pad_lanes(v):
    return jnp.pad(v.astype(F32), (0, LANES - v.shape[0]))[None, :]


def kernel(x_prompt, x_sample, state_gdn_S, state_gdn_conv, state_conf_conv, meta_tokens, norm_w, w_in,
           gdn_conv_w, gdn_A_log, gdn_dt_bias, gdn_norm_w, conf_dw_w, conf_dw_b, conf_ln_w, conf_ln_b,
           w_out, final_norm_w):
    bsz, seq, _ = x_prompt.shape
    dec = x_sample.shape[0]
    l = 0
    wi = w_in[l]
    w_all = jnp.concatenate(
        [wi[:, :QKV_WIDTH], wi[:, QKV_WIDTH + 2 * HEADS:], wi[:, QKV_WIDTH:QKV_WIDTH + 2 * HEADS],
         jnp.zeros((D_MODEL, LANES - 2 * HEADS), wi.dtype)], axis=1).astype(BF16)
    w_o = w_out[l].astype(BF16)
    nw = norm_w[l][None, :]
    fw = final_norm_w[None, :]
    wts = (gdn_conv_w[l], _pad_lanes(gdn_A_log[l]), _pad_lanes(gdn_dt_bias[l]), gdn_norm_w[l][None, :],
           conf_dw_w[l], conf_dw_b[l][None, :], conf_ln_w[l][None, :], conf_ln_b[l][None, :])

    xp = x_prompt.reshape(bsz * seq, D_MODEL)
    xs = x_sample.reshape(dec, D_MODEL)
    x_small = jnp.concatenate([xs, meta_tokens.astype(xs.dtype)], axis=0)

    p_prompt = _matmul(_norm_cast(xp, nw, 512), w_all, 512, P_COLS // 3)
    p_small = _matmul(_norm_cast(x_small, nw, dec + N_META), w_all, dec + N_META, P_COLS // 3)

    zs = jnp.zeros((HEADS, HEAD_K, HEAD_V), F32)
    zg = jnp.zeros((GHIST_ROWS, QKV_WIDTH), F32)
    zc = jnp.zeros((CHIST_ROWS, CONF_WIDTH), F32)
    _, _, s_m, g_m, c_m = _seq_call(p_small[dec:][None], zs, zg, zc, wts, N_META, N_META)

    gdn_p, conf_p, s_p, g_p, c_p = _seq_call(p_prompt.reshape(bsz, seq, P_COLS), s_m[0], g_m[0], c_m[0],
                                             wts, 2 * GDN_CHUNK, GDN_CHUNK)
    y_prompt = _out_call(gdn_p.reshape(bsz * seq, GDN_WIDTH), conf_p.reshape(bsz * seq, CONF_WIDTH),
                         w_o, xp, fw, 512).reshape(bsz, seq, D_MODEL)

    sdt = state_gdn_S.dtype
    gdn_s, conf_s, s_s, g_s, c_s = _step_call(p_small[:dec], state_gdn_S[l], state_gdn_conv[l],
                                              state_conf_conv[l], wts, SUBLANES, sdt)
    y_sample = _out_call(gdn_s, conf_s, w_o, xs, fw, dec).reshape(dec, 1, D_MODEL)

    nh = SHORT_CONV - 1
    nc = CONF_KERNEL - 1
    return (y_prompt, y_sample,
            s_p.astype(sdt)[None],
            g_p[:, GHIST_ROWS - nh:, :].astype(state_gdn_conv.dtype)[None],
            c_p[:, CHIST_ROWS - nc:, :].astype(state_conf_conv.dtype)[None],
            s_s[None], g_s[None], c_s[None])
```

```python
import functools

import jax
import jax.numpy as jnp
from jax import lax
from jax.experimental import pallas as pl
from jax.experimental.pallas import tpu as pltpu

F32 = jnp.float32
BF16 = jnp.bfloat16

D_MODEL = 2048
N_META = 16
HEADS = 8
HEAD_K = 128
HEAD_V = 128
GDN_WIDTH = HEADS * HEAD_V
CONF_WIDTH = 1024
QKV_WIDTH = 3 * GDN_WIDTH
SHORT_CONV = 4
CONF_KERNEL = 31
GDN_CHUNK = 64
RMS_EPS = 1e-6
LN_EPS = 1e-5

C_QKV = 0
C_ZG = QKV_WIDTH
C_GA = C_ZG + GDN_WIDTH
C_GB = C_GA + CONF_WIDTH
C_ZC = C_GB + CONF_WIDTH
C_AB = C_ZC + CONF_WIDTH
LANES = 128
SUBLANES = 8
P_COLS = C_AB + LANES

GHIST_ROWS = SUBLANES
CHIST_ROWS = 32

VMEM_LIMIT = 56 * 1024 * 1024

INV_PASSES = 3


def _silu(x):
    return x * jax.nn.sigmoid(x)


def _softplus(x):
    return jnp.maximum(x, 0.0) + jnp.log1p(jnp.exp(-jnp.abs(x)))


def _split_bf16(x):
    hi = x.astype(BF16)
    lo = (x - hi.astype(F32)).astype(BF16)
    return hi, lo


def _mm(a, b, passes=1):
    if passes == 1:
        return jnp.dot(a.astype(BF16), b.astype(BF16), preferred_element_type=F32)
    a_hi, a_lo = _split_bf16(a)
    b_hi, b_lo = _split_bf16(b)
    return (jnp.dot(a_hi, b_hi, preferred_element_type=F32)
            + (jnp.dot(a_hi, b_lo, preferred_element_type=F32)
               + jnp.dot(a_lo, b_hi, preferred_element_type=F32)))


def _mm_nt(a, b):
    return lax.dot_general(a.astype(BF16), b.astype(BF16), (((1,), (1,)), ((), ())),
                           preferred_element_type=F32)


def _mm_tn(a, b):
    return lax.dot_general(a.astype(BF16), b.astype(BF16), (((0,), (0,)), ((), ())),
                           preferred_element_type=F32)


def _norm_cast_kernel(x_ref, w_ref, o_ref):
    x = x_ref[...]
    y = x * lax.rsqrt(jnp.mean(x * x, axis=-1, keepdims=True) + RMS_EPS)
    o_ref[...] = (y * w_ref[...]).astype(BF16)


def _norm_cast(x, w, tm):
    m = x.shape[0]
    return pl.pallas_call(
        _norm_cast_kernel,
        grid=(m // tm,),
        in_specs=[pl.BlockSpec((tm, D_MODEL), lambda i: (i, 0)),
                  pl.BlockSpec((1, D_MODEL), lambda i: (0, 0))],
        out_specs=pl.BlockSpec((tm, D_MODEL), lambda i: (i, 0)),
        out_shape=jax.ShapeDtypeStruct((m, D_MODEL), BF16),
        compiler_params=pltpu.CompilerParams(dimension_semantics=("parallel",),
                                             vmem_limit_bytes=VMEM_LIMIT),
        name="norm_cast",
    )(x, w)


def _matmul_kernel(x_ref, w_ref, o_ref):
    o_ref[...] = jnp.dot(x_ref[...], w_ref[...], preferred_element_type=F32)


def _matmul(x, w, tm, tn):
    m, k = x.shape
    n = w.shape[1]
    return pl.pallas_call(
        _matmul_kernel,
        grid=(n // tn, m // tm),
        in_specs=[pl.BlockSpec((tm, k), lambda j, i: (i, 0)),
                  pl.BlockSpec((k, tn), lambda j, i: (0, j))],
        out_specs=pl.BlockSpec((tm, tn), lambda j, i: (i, j)),
        out_shape=jax.ShapeDtypeStruct((m, n), F32),
        compiler_params=pltpu.CompilerParams(dimension_semantics=("parallel", "parallel"),
                                             vmem_limit_bytes=VMEM_LIMIT),
        name="in_proj",
    )(x, w)


def _tri_inverse_many(a_list, eye, blk_masks, passes):
    d0 = [jnp.where(blk_masks[0], a, 0.0) for a in a_list]
    x = [eye - d for d in d0]
    d2 = [_mm(d, d, passes) for d in d0]
    x = [xi + _mm(xi, di, passes) for xi, di in zip(x, d2)]
    d4 = [_mm(d, d, passes) for d in d2]
    x = [xi + _mm(xi, di, passes) for xi, di in zip(x, d4)]
    for lvl in range(1, len(blk_masks)):
        off_mask = blk_masks[lvl] & jnp.logical_not(blk_masks[lvl - 1])
        off = [jnp.where(off_mask, a, 0.0) for a in a_list]
        y = [_mm(xi, oi, passes) for xi, oi in zip(x, off)]
        x = [xi - _mm(yi, xi, passes) for xi, yi in zip(x, y)]
    return x


def _seq_kernel(p_ref, s0_ref, g0_ref, c0_ref, cw_ref, alog_ref, dtb_ref, gnw_ref,
                dww_ref, dwb_ref, lnw_ref, lnb_ref,
                gdn_ref, conf_ref, s_out_ref, g_out_ref, c_out_ref,
                s_scr, xbuf, cbuf, shbuf, *, t, c):
    step = pl.program_id(1)
    last = pl.num_programs(1) - 1
    nch = t // c

    @pl.when(step == 0)
    def _():
        s_scr[...] = s0_ref[...]
        xbuf[0:GHIST_ROWS, :] = g0_ref[...]
        cbuf[0:CHIST_ROWS, :] = c0_ref[...]

    xbuf[GHIST_ROWS:GHIST_ROWS + t, :] = p_ref[0, :, C_QKV:C_QKV + QKV_WIDTH]
    base = GHIST_ROWS - (SHORT_CONV - 1)

    def conv_block(lo):
        sl = slice(lo, lo + LANES)
        acc = cw_ref[0:1, sl] * xbuf[base:base + t, sl]
        for k in range(1, SHORT_CONV):
            acc = acc + cw_ref[k:k + 1, sl] * xbuf[base + k:base + k + t, sl]
        return _silu(acc)

    qn, kn, vv = {}, {}, {}
    for h in range(HEADS):
        qf = conv_block(h * HEAD_K)
        kf = conv_block(GDN_WIDTH + h * HEAD_K)
        vf = conv_block(2 * GDN_WIDTH + h * HEAD_V)
        qf = qf * lax.rsqrt(jnp.sum(qf * qf, axis=-1, keepdims=True) + RMS_EPS) * (HEAD_K ** -0.5)
        kf = kf * lax.rsqrt(jnp.sum(kf * kf, axis=-1, keepdims=True) + RMS_EPS)
        for ci in range(nch):
            rows = slice(ci * c, (ci + 1) * c)
            qn[ci, h], kn[ci, h], vv[ci, h] = qf[rows], kf[rows], vf[rows]
    new_hist = xbuf[t:t + GHIST_ROWS, :]
    xbuf[0:GHIST_ROWS, :] = new_hist

    ab = p_ref[0, :, C_AB:C_AB + LANES]
    g_all = -jnp.exp(alog_ref[...]) * _softplus(ab + dtb_ref[...])
    beta_all = jax.nn.sigmoid(ab)

    r = lax.broadcasted_iota(jnp.int32, (c, c), 0)
    q = lax.broadcasted_iota(jnp.int32, (c, c), 1)
    incl = r >= q
    strict = r > q
    eye = (r == q).astype(F32)
    tril = incl.astype(BF16)
    blk_masks = []
    b = 8
    while b <= c:
        sh = b.bit_length() - 1
        blk_masks.append((r >> sh) == (q >> sh))
        b *= 2

    gcum, gcum_t, beta = [], [], []
    for ci in range(nch):
        g = g_all[ci * c:(ci + 1) * c]
        g_hi = g.astype(BF16)
        r1 = g - g_hi.astype(F32)
        g_mid = r1.astype(BF16)
        g_lo = (r1 - g_mid.astype(F32)).astype(BF16)
        gc = (jnp.dot(tril, g_hi, preferred_element_type=F32)
              + (jnp.dot(tril, g_mid, preferred_element_type=F32)
                 + jnp.dot(tril, g_lo, preferred_element_type=F32)))
        gcum.append(gc)
        gcum_t.append(gc.T)
        beta.append(beta_all[ci * c:(ci + 1) * c])

    keys = [(ci, h) for ci in range(nch) for h in range(HEADS)]
    gc_col = {k: gcum[k[0]][:, k[1]:k[1] + 1] for k in keys}
    beta_col = {k: beta[k[0]][:, HEADS + k[1]:HEADS + k[1] + 1] for k in keys}
    gam = {k: jnp.where(incl, jnp.exp(jnp.where(incl, gc_col[k] - gcum_t[k[0]][k[1]:k[1] + 1, :], 0.0)), 0.0)
           for k in keys}
    kq = {k: _mm_nt(jnp.concatenate([kn[k], qn[k]], axis=0), kn[k]) for k in keys}
    a = {k: jnp.where(strict, beta_col[k] * kq[k][:c] * gam[k], 0.0) for k in keys}
    qkg = {k: kq[k][c:] * gam[k] for k in keys}
    tinv = dict(zip(keys, _tri_inverse_many([a[k] for k in keys], eye, blk_masks, INV_PASSES)))
    eg = {k: jnp.exp(gc_col[k]) for k in keys}
    sol = {k: _mm(tinv[k], jnp.concatenate([beta_col[k] * vv[k], (beta_col[k] * eg[k]) * kn[k]], axis=-1),
                  INV_PASSES) for k in keys}
    g_last = {k: gc_col[k][c - 1:c, :] for k in keys}
    kd = {k: kn[k] * jnp.exp(g_last[k] - gc_col[k]) for k in keys}

    gnw = gnw_ref[...]
    s = [s_scr[h] for h in range(HEADS)]
    for ci in range(nch):
        rows = slice(ci * c, (ci + 1) * c)
        ks = [(ci, h) for h in range(HEADS)]
        ws = [_mm(jnp.concatenate([sol[k][:, HEAD_V:], qn[k]], axis=0), s[k[1]]) for k in ks]
        v_new = [sol[k][:, :HEAD_V] - w[:c] for k, w in zip(ks, ws)]
        o = [eg[k] * w[c:] + _mm(qkg[k], vn) for k, w, vn in zip(ks, ws, v_new)]
        s = [jnp.exp(g_last[k]) * s[k[1]] + _mm_tn(kd[k], vn) for k, vn in zip(ks, v_new)]
        for h in range(HEADS):
            on = o[h] * lax.rsqrt(jnp.mean(o[h] * o[h], axis=-1, keepdims=True) + RMS_EPS) * gnw
            z = p_ref[0, rows, C_ZG + h * HEAD_V:C_ZG + (h + 1) * HEAD_V]
            gdn_ref[0, rows, h * HEAD_V:(h + 1) * HEAD_V] = (on * _silu(z)).astype(gdn_ref.dtype)
    for h in range(HEADS):
        s_scr[h] = s[h]

    glu = p_ref[0, :, C_GA:C_GA + CONF_WIDTH] * jax.nn.sigmoid(p_ref[0, :, C_GB:C_GB + CONF_WIDTH])
    cbuf[CHIST_ROWS:CHIST_ROWS + t, :] = glu
    cbase = CHIST_ROWS - (CONF_KERNEL - 1)
    span = t + CHIST_ROWS - SUBLANES
    for sft in range(1, SUBLANES):
        shbuf[sft - 1] = cbuf[sft:sft + span, :]
    cv_blocks = []
    for cb in range(CONF_WIDTH // LANES):
        sl = slice(cb * LANES, (cb + 1) * LANES)
        acc = None
        for k in range(CONF_KERNEL):
            m, sft = divmod(cbase + k, SUBLANES)
            if sft == 0:
                xs = cbuf[SUBLANES * m:SUBLANES * m + t, sl]
            else:
                xs = shbuf[sft - 1, SUBLANES * m:SUBLANES * m + t, sl]
            term = dww_ref[k:k + 1, sl] * xs
            acc = term if acc is None else acc + term
        cv_blocks.append(acc + dwb_ref[:, sl])
    cv = jnp.concatenate(cv_blocks, axis=-1)
    mu = jnp.mean(cv, axis=-1, keepdims=True)
    cc = cv - mu
    cn = cc * lax.rsqrt(jnp.mean(cc * cc, axis=-1, keepdims=True) + LN_EPS)
    cn = cn * lnw_ref[...] + lnb_ref[...]
    zc = p_ref[0, :, C_ZC:C_ZC + CONF_WIDTH]
    conf_ref[0] = (_silu(cn) * _silu(zc)).astype(conf_ref.dtype)
    new_chist = cbuf[t:t + CHIST_ROWS, :]
    cbuf[0:CHIST_ROWS, :] = new_chist

    @pl.when(step == last)
    def _():
        s_out_ref[0] = s_scr[...]
        g_out_ref[0] = xbuf[0:GHIST_ROWS, :]
        c_out_ref[0] = cbuf[0:CHIST_ROWS, :]


def _seq_call(p, s0, g0, c0, wts, t, c):
    bsz, seq, _ = p.shape
    cw, alog, dtb, gnw, dww, dwb, lnw, lnb = wts
    full = lambda shape: pl.BlockSpec(shape, lambda b, s: (0,) * len(shape))
    return pl.pallas_call(
        functools.partial(_seq_kernel, t=t, c=c),
        grid=(bsz, seq // t),
        in_specs=[pl.BlockSpec((1, t, P_COLS), lambda b, s: (b, s, 0)),
                  full((HEADS, HEAD_K, HEAD_V)),
                  full((GHIST_ROWS, QKV_WIDTH)),
                  full((CHIST_ROWS, CONF_WIDTH)),
                  full(cw.shape), full(alog.shape), full(dtb.shape), full(gnw.shape),
                  full(dww.shape), full(dwb.shape), full(lnw.shape), full(lnb.shape)],
        out_specs=[pl.BlockSpec((1, t, GDN_WIDTH), lambda b, s: (b, s, 0)),
                   pl.BlockSpec((1, t, CONF_WIDTH), lambda b, s: (b, s, 0)),
                   pl.BlockSpec((1, HEADS, HEAD_K, HEAD_V), lambda b, s: (b, 0, 0, 0)),
                   pl.BlockSpec((1, GHIST_ROWS, QKV_WIDTH), lambda b, s: (b, 0, 0)),
                   pl.BlockSpec((1, CHIST_ROWS, CONF_WIDTH), lambda b, s: (b, 0, 0))],
        out_shape=[jax.ShapeDtypeStruct((bsz, seq, GDN_WIDTH), BF16),
                   jax.ShapeDtypeStruct((bsz, seq, CONF_WIDTH), BF16),
                   jax.ShapeDtypeStruct((bsz, HEADS, HEAD_K, HEAD_V), F32),
                   jax.ShapeDtypeStruct((bsz, GHIST_ROWS, QKV_WIDTH), F32),
                   jax.ShapeDtypeStruct((bsz, CHIST_ROWS, CONF_WIDTH), F32)],
        scratch_shapes=[pltpu.VMEM((HEADS, HEAD_K, HEAD_V), F32),
                        pltpu.VMEM((t + GHIST_ROWS, QKV_WIDTH), F32),
                        pltpu.VMEM((t + CHIST_ROWS, CONF_WIDTH), F32),
                        pltpu.VMEM((SUBLANES - 1, t + CHIST_ROWS - SUBLANES, CONF_WIDTH), F32)],
        compiler_params=pltpu.CompilerParams(dimension_semantics=("parallel", "arbitrary"),
                                             vmem_limit_bytes=VMEM_LIMIT),
        name=f"seq_t{t}",
    )(p, s0, g0, c0, cw, alog, dtb, gnw, dww, dwb, lnw, lnb)


def _step_kernel(p_ref, s_ref, gh_ref, ch_ref, cw_ref, alog_ref, dtb_ref, gnw_ref,
                 dww_ref, dwb_ref, lnw_ref, lnb_ref,
                 gdn_ref, conf_ref, s_out_ref, g_out_ref, c_out_ref,
                 conv_scr, o_scr, cv_scr, *, nb):
    nh = SHORT_CONV - 1
    nc = CONF_KERNEL - 1
    for b in range(nb):
        x = p_ref[b:b + 1, C_QKV:C_QKV + QKV_WIDTH]
        hist = gh_ref[b]
        conv_scr[b:b + 1, :] = (jnp.sum(cw_ref[0:nh, :] * hist, axis=0, keepdims=True)
                                + cw_ref[nh:nh + 1, :] * x)
        g_out_ref[b, 0:nh - 1, :] = hist[1:nh]
        g_out_ref[b, nh - 1:nh, :] = x
        glu = (p_ref[b:b + 1, C_GA:C_GA + CONF_WIDTH]
               * jax.nn.sigmoid(p_ref[b:b + 1, C_GB:C_GB + CONF_WIDTH]))
        chist = ch_ref[b]
        cv_scr[b:b + 1, :] = (jnp.sum(dww_ref[0:nc, :] * chist, axis=0, keepdims=True)
                              + dww_ref[nc:nc + 1, :] * glu)
        c_out_ref[b, 0:nc - 1, :] = chist[1:nc]
        c_out_ref[b, nc - 1:nc, :] = glu

    qkv = _silu(conv_scr[...])
    ab = p_ref[:, C_AB:C_AB + LANES]
    g_all = -jnp.exp(alog_ref[...]) * _softplus(ab + dtb_ref[...])
    eg_all = jnp.exp(g_all)
    beta_all = jax.nn.sigmoid(ab)
    pad = jnp.zeros((HEAD_K - nb, HEAD_K), F32)

    for h in range(HEADS):
        qh = qkv[:, h * HEAD_K:(h + 1) * HEAD_K]
        kh = qkv[:, GDN_WIDTH + h * HEAD_K:GDN_WIDTH + (h + 1) * HEAD_K]
        vh = qkv[:, 2 * GDN_WIDTH + h * HEAD_V:2 * GDN_WIDTH + (h + 1) * HEAD_V]
        qn = qh * lax.rsqrt(jnp.sum(qh * qh, axis=-1, keepdims=True) + RMS_EPS) * (HEAD_K ** -0.5)
        kn = kh * lax.rsqrt(jnp.sum(kh * kh, axis=-1, keepdims=True) + RMS_EPS)
        qk = jnp.sum(qn * kn, axis=-1, keepdims=True)
        kt = jnp.concatenate([kn, pad], axis=0).T
        qt = jnp.concatenate([qn, pad], axis=0).T
        for b in range(nb):
            s = s_ref[b, h]
            kcol = kt[:, b:b + 1]
            qcol = qt[:, b:b + 1]
            ks = jnp.sum(kcol * s, axis=0, keepdims=True)
            qs = jnp.sum(qcol * s, axis=0, keepdims=True)
            eg = eg_all[b:b + 1, h:h + 1]
            beta = beta_all[b:b + 1, HEADS + h:HEADS + h + 1]
            v_new = beta * vh[b:b + 1, :] - (beta * eg) * ks
            o_scr[b:b + 1, h * HEAD_V:(h + 1) * HEAD_V] = eg * qs + qk[b:b + 1, :] * v_new
            s_out_ref[b, h] = eg * s + kcol * v_new

    gnw = gnw_ref[...]
    for h in range(HEADS):
        o = o_scr[:, h * HEAD_V:(h + 1) * HEAD_V]
        on = o * lax.rsqrt(jnp.mean(o * o, axis=-1, keepdims=True) + RMS_EPS) * gnw
        z = p_ref[:, C_ZG + h * HEAD_V:C_ZG + (h + 1) * HEAD_V]
        gdn_ref[:, h * HEAD_V:(h + 1) * HEAD_V] = (on * _silu(z)).astype(gdn_ref.dtype)

    cv = cv_scr[...] + dwb_ref[...]
    mu = jnp.mean(cv, axis=-1, keepdims=True)
    cc = cv - mu
    cn = cc * lax.rsqrt(jnp.mean(cc * cc, axis=-1, keepdims=True) + LN_EPS)
    cn = cn * lnw_ref[...] + lnb_ref[...]
    conf_ref[...] = (_silu(cn) * _silu(p_ref[:, C_ZC:C_ZC + CONF_WIDTH])).astype(conf_ref.dtype)


def _step_call(p, s, gh, ch, wts, nb, out_dtype):
    bsz = p.shape[0]
    cw, alog, dtb, gnw, dww, dwb, lnw, lnb = wts
    full = lambda shape: pl.BlockSpec(shape, lambda i: (0,) * len(shape))
    nh = SHORT_CONV - 1
    nc = CONF_KERNEL - 1
    return pl.pallas_call(
        functools.partial(_step_kernel, nb=nb),
        grid=(bsz // nb,),
        in_specs=[pl.BlockSpec((nb, P_COLS), lambda i: (i, 0)),
                  pl.BlockSpec((nb, HEADS, HEAD_K, HEAD_V), lambda i: (i, 0, 0, 0)),
                  pl.BlockSpec((nb, nh, QKV_WIDTH), lambda i: (i, 0, 0)),
                  pl.BlockSpec((nb, nc, CONF_WIDTH), lambda i: (i, 0, 0)),
                  full(cw.shape), full(alog.shape), full(dtb.shape), full(gnw.shape),
                  full(dww.shape), full(dwb.shape), full(lnw.shape), full(lnb.shape)],
        out_specs=[pl.BlockSpec((nb, GDN_WIDTH), lambda i: (i, 0)),
                   pl.BlockSpec((nb, CONF_WIDTH), lambda i: (i, 0)),
                   pl.BlockSpec((nb, HEADS, HEAD_K, HEAD_V), lambda i: (i, 0, 0, 0)),
                   pl.BlockSpec((nb, nh, QKV_WIDTH), lambda i: (i, 0, 0)),
                   pl.BlockSpec((nb, nc, CONF_WIDTH), lambda i: (i, 0, 0))],
        out_shape=[jax.ShapeDtypeStruct((bsz, GDN_WIDTH), BF16),
                   jax.ShapeDtypeStruct((bsz, CONF_WIDTH), BF16),
                   jax.ShapeDtypeStruct(s.shape, out_dtype),
                   jax.ShapeDtypeStruct(gh.shape, out_dtype),
                   jax.ShapeDtypeStruct(ch.shape, out_dtype)],
        scratch_shapes=[pltpu.VMEM((nb, QKV_WIDTH), F32),
                        pltpu.VMEM((nb, GDN_WIDTH), F32),
                        pltpu.VMEM((nb, CONF_WIDTH), F32)],
        compiler_params=pltpu.CompilerParams(dimension_semantics=("parallel",),
                                             vmem_limit_bytes=VMEM_LIMIT),
        name="decode_step",
    )(p, s, gh, ch, cw, alog, dtb, gnw, dww, dwb, lnw, lnb)


def _out_kernel(gdn_ref, conf_ref, w_ref, x_ref, fw_ref, o_ref):
    y = (jnp.dot(gdn_ref[...], w_ref[0:GDN_WIDTH, :], preferred_element_type=F32)
         + jnp.dot(conf_ref[...], w_ref[GDN_WIDTH:, :], preferred_element_type=F32))
    hres = x_ref[...] + y
    o_ref[...] = hres * lax.rsqrt(jnp.mean(hres * hres, axis=-1, keepdims=True) + RMS_EPS) * fw_ref[...]


def _out_call(gdn, conf, w, x, fw, tm):
    m = x.shape[0]
    return pl.pallas_call(
        _out_kernel,
        grid=(m // tm,),
        in_specs=[pl.BlockSpec((tm, GDN_WIDTH), lambda i: (i, 0)),
                  pl.BlockSpec((tm, CONF_WIDTH), lambda i: (i, 0)),
                  pl.BlockSpec(w.shape, lambda i: (0, 0)),
                  pl.BlockSpec((tm, D_MODEL), lambda i: (i, 0)),
                  pl.BlockSpec((1, D_MODEL), lambda i: (0, 0))],
        out_specs=pl.BlockSpec((tm, D_MODEL), lambda i: (i, 0)),
        out_shape=jax.ShapeDtypeStruct((m, D_MODEL), F32),
        compiler_params=pltpu.CompilerParams(dimension_semantics=("parallel",),
                                             vmem_limit_bytes=VMEM_LIMIT),
        name="out_proj",
    )(gdn, conf, w, x, fw)


def _pad_lanes(v):
    return jnp.pad(v.astype(F32), (0, LANES - v.shape[0]))[None, :]


def kernel(x_prompt, x_sample, state_gdn_S, state_gdn_conv, state_conf_conv, meta_tokens, norm_w, w_in,
           gdn_conv_w, gdn_A_log, gdn_dt_bias, gdn_norm_w, conf_dw_w, conf_dw_b, conf_ln_w, conf_ln_b,
           w_out, final_norm_w):
    bsz, seq, _ = x_prompt.shape
    dec = x_sample.shape[0]
    l = 0
    wi = w_in[l]
    w_all = jnp.concatenate(
        [wi[:, :QKV_WIDTH], wi[:, QKV_WIDTH + 2 * HEADS:], wi[:, QKV_WIDTH:QKV_WIDTH + 2 * HEADS],
         jnp.zeros((D_MODEL, LANES - 2 * HEADS), wi.dtype)], axis=1).astype(BF16)
    w_o = w_out[l].astype(BF16)
    nw = norm_w[l][None, :]
    fw = final_norm_w[None, :]
    wts = (gdn_conv_w[l], _pad_lanes(gdn_A_log[l]), _pad_lanes(gdn_dt_bias[l]), gdn_norm_w[l][None, :],
           conf_dw_w[l], conf_dw_b[l][None, :], conf_ln_w[l][None, :], conf_ln_b[l][None, :])

    xp = x_prompt.reshape(bsz * seq, D_MODEL)
    xs = x_sample.reshape(dec, D_MODEL)
    x_small = jnp.concatenate([xs, meta_tokens.astype(xs.dtype)], axis=0)

    p_prompt = _matmul(_norm_cast(xp, nw, 512), w_all, 512, P_COLS // 3)
    p_small = _matmul(_norm_cast(x_small, nw, dec + N_META), w_all, dec + N_META, P_COLS // 3)

    zs = jnp.zeros((HEADS, HEAD_K, HEAD_V), F32)
    zg = jnp.zeros((GHIST_ROWS, QKV_WIDTH), F32)
    zc = jnp.zeros((CHIST_ROWS, CONF_WIDTH), F32)
    _, _, s_m, g_m, c_m = _seq_call(p_small[dec:][None], zs, zg, zc, wts, N_META, N_META)

    gdn_p, conf_p, s_p, g_p, c_p = _seq_call(p_prompt.reshape(bsz, seq, P_COLS), s_m[0], g_m[0], c_m[0],
                                             wts, 2 * GDN_CHUNK, GDN_CHUNK)
    y_prompt = _out_call(gdn_p.reshape(bsz * seq, GDN_WIDTH), conf_p.reshape(bsz * seq, CONF_WIDTH),
                         w_o, xp, fw, 512).reshape(bsz, seq, D_MODEL)

    sdt = state_gdn_S.dtype
    gdn_s, conf_s, s_s, g_s, c_s = _step_call(p_small[:dec], state_gdn_S[l], state_gdn_conv[l],
                                              state_conf_conv[l], wts, SUBLANES, sdt)
    y_sample = _out_call(gdn_s, conf_s, w_o, xs, fw, dec).reshape(dec, 1, D_MODEL)

    nh = SHORT_CONV - 1
    nc = CONF_KERNEL - 1
    return (y_prompt, y_sample,
            s_p.astype(sdt)[None],
            g_p[:, GHIST_ROWS - nh:, :].astype(state_gdn_conv.dtype)[None],
            c_p[:, CHIST_ROWS - nc:, :].astype(state_conf_conv.dtype)[None],
            s_s[None], g_s[None], c_s[None])
```

```python
import functools

import jax
import jax.numpy as jnp
from jax import lax
from jax.experimental import pallas as pl
from jax.experimental.pallas import tpu as pltpu

F32 = jnp.float32
BF16 = jnp.bfloat16

D_MODEL = 2048
N_META = 16
HEADS = 8
HEAD_K = 128
HEAD_V = 128
GDN_WIDTH = HEADS * HEAD_V
CONF_WIDTH = 1024
QKV_WIDTH = 3 * GDN_WIDTH
SHORT_CONV = 4
CONF_KERNEL = 31
GDN_CHUNK = 64
RMS_EPS = 1e-6
LN_EPS = 1e-5

LANES = 128
SUBLANES = 8

C_QKV = 0
C_ZG = QKV_WIDTH
C_GA = C_ZG + GDN_WIDTH
C_GB = C_GA + CONF_WIDTH
C_ZC = C_GB + CONF_WIDTH
P_COLS = C_ZC + CONF_WIDTH
N_GATES = 2 * HEADS
IN_TILE = 1024

GHIST_ROWS = SUBLANES
CHIST_ROWS = 32

VMEM_LIMIT = 56 * 1024 * 1024


def _silu(x):
    return x * jax.nn.sigmoid(x)


def _softplus(x):
    return jnp.maximum(x, 0.0) + jnp.log1p(jnp.exp(-jnp.abs(x)))


def _mm(a, b):
    return jnp.dot(a.astype(BF16), b.astype(BF16), preferred_element_type=F32)


def _mm_nt(a, b):
    return lax.dot_general(a.astype(BF16), b.astype(BF16), (((1,), (1,)), ((), ())),
                           preferred_element_type=F32)


def _mm_tn(a, b):
    return lax.dot_general(a.astype(BF16), b.astype(BF16), (((0,), (0,)), ((), ())),
                           preferred_element_type=F32)


def _norm_cast_kernel(x_ref, w_ref, o_ref):
    x = x_ref[...]
    y = x * lax.rsqrt(jnp.mean(x * x, axis=-1, keepdims=True) + RMS_EPS)
    o_ref[...] = (y * w_ref[...]).astype(BF16)


def _norm_cast(x, w, tm):
    m = x.shape[0]
    return pl.pallas_call(
        _norm_cast_kernel,
        grid=(m // tm,),
        in_specs=[pl.BlockSpec((tm, D_MODEL), lambda i: (i, 0)),
                  pl.BlockSpec((1, D_MODEL), lambda i: (0, 0))],
        out_specs=pl.BlockSpec((tm, D_MODEL), lambda i: (i, 0)),
        out_shape=jax.ShapeDtypeStruct((m, D_MODEL), BF16),
        compiler_params=pltpu.CompilerParams(dimension_semantics=("parallel",),
                                             vmem_limit_bytes=VMEM_LIMIT),
        name="norm_cast",
    )(x, w)


def _in_proj_kernel(x_ref, wt_ref, o_ref, wb_scr):
    @pl.when(pl.program_id(1) == 0)
    def _():
        wb_scr[...] = wt_ref[...].astype(BF16)

    o_ref[...] = lax.dot_general(x_ref[...], wb_scr[...], (((1,), (1,)), ((), ())),
                                 preferred_element_type=F32)


def _in_proj(x, wt, row_start, n_tiles, tn, tm, name):
    m, k = x.shape
    return pl.pallas_call(
        _in_proj_kernel,
        grid=(n_tiles, m // tm),
        in_specs=[pl.BlockSpec((tm, k), lambda j, i: (i, 0)),
                  pl.BlockSpec((pl.Element(tn), pl.Element(k)), lambda j, i: (row_start(j), 0))],
        out_specs=pl.BlockSpec((tm, tn), lambda j, i: (i, j)),
        out_shape=jax.ShapeDtypeStruct((m, n_tiles * tn), F32),
        scratch_shapes=[pltpu.VMEM((tn, k), BF16)],
        compiler_params=pltpu.CompilerParams(dimension_semantics=("arbitrary", "arbitrary"),
                                             vmem_limit_bytes=VMEM_LIMIT),
        name=name,
    )(x, wt)


def _main_row_start(j):
    return (j * (IN_TILE // N_GATES) + jnp.where(j >= QKV_WIDTH // IN_TILE, 1, 0)) * N_GATES


def _gate_row_start(j):
    return (QKV_WIDTH // N_GATES + 0 * j) * N_GATES


def _tri_inverse_many(a_list, eye, blk_masks):
    d0 = [jnp.where(blk_masks[0], a, 0.0) for a in a_list]
    x = [eye - d for d in d0]
    d2 = [_mm(d, d) for d in d0]
    x = [xi + _mm(xi, di) for xi, di in zip(x, d2)]
    d4 = [_mm(d, d) for d in d2]
    x = [xi + _mm(xi, di) for xi, di in zip(x, d4)]
    for lvl in range(1, len(blk_masks)):
        off_mask = blk_masks[lvl] & jnp.logical_not(blk_masks[lvl - 1])
        off = [jnp.where(off_mask, a, 0.0) for a in a_list]
        y = [_mm(xi, oi) for xi, oi in zip(x, off)]
        x = [xi - _mm(yi, xi) for xi, yi in zip(x, y)]
    return x


def _seq_kernel(p_ref, ab_ref, s0_ref, g0_ref, c0_ref, cw_ref, alog_ref, dtb_ref, gnw_ref,
                dww_ref, dwb_ref, lnw_ref, lnb_ref,
                gdn_ref, conf_ref, s_out_ref, g_out_ref, c_out_ref,
                s_scr, xbuf, cbuf, shbuf, *, t, c):
    step = pl.program_id(1)
    last = pl.num_programs(1) - 1
    nch = t // c

    @pl.when(step == 0)
    def _():
        s_scr[...] = s0_ref[...]
        xbuf[0:GHIST_ROWS, :] = g0_ref[...]
        cbuf[0:CHIST_ROWS, :] = c0_ref[...]

    xbuf[GHIST_ROWS:GHIST_ROWS + t, :] = p_ref[0, :, C_QKV:C_QKV + QKV_WIDTH]
    base = GHIST_ROWS - (SHORT_CONV - 1)

    def conv_block(lo):
        sl = slice(lo, lo + LANES)
        acc = cw_ref[0:1, sl] * xbuf[base:base + t, sl]
        for k in range(1, SHORT_CONV):
            acc = acc + cw_ref[k:k + 1, sl] * xbuf[base + k:base + k + t, sl]
        return _silu(acc)

    qn, kn, vv = {}, {}, {}
    for h in range(HEADS):
        qf = conv_block(h * HEAD_K)
        kf = conv_block(GDN_WIDTH + h * HEAD_K)
        vf = conv_block(2 * GDN_WIDTH + h * HEAD_V)
        qf = qf * lax.rsqrt(jnp.sum(qf * qf, axis=-1, keepdims=True) + RMS_EPS) * (HEAD_K ** -0.5)
        kf = kf * lax.rsqrt(jnp.sum(kf * kf, axis=-1, keepdims=True) + RMS_EPS)
        for ci in range(nch):
            rows = slice(ci * c, (ci + 1) * c)
            qn[ci, h], kn[ci, h], vv[ci, h] = qf[rows], kf[rows], vf[rows]
    new_hist = xbuf[t:t + GHIST_ROWS, :]
    xbuf[0:GHIST_ROWS, :] = new_hist

    ab = ab_ref[0]
    g_all = -jnp.exp(alog_ref[...]) * _softplus(ab + dtb_ref[...])
    beta_all = jax.nn.sigmoid(ab)

    r = lax.broadcasted_iota(jnp.int32, (c, c), 0)
    q = lax.broadcasted_iota(jnp.int32, (c, c), 1)
    incl = r >= q
    strict = r > q
    eye = (r == q).astype(F32)
    tril = incl.astype(BF16)
    blk_masks = []
    b = 8
    while b <= c:
        sh = b.bit_length() - 1
        blk_masks.append((r >> sh) == (q >> sh))
        b *= 2

    gcum, gcum_t, beta = [], [], []
    for ci in range(nch):
        g = g_all[ci * c:(ci + 1) * c]
        g_hi = g.astype(BF16)
        r1 = g - g_hi.astype(F32)
        g_mid = r1.astype(BF16)
        g_lo = (r1 - g_mid.astype(F32)).astype(BF16)
        gc = (jnp.dot(tril, g_hi, preferred_element_type=F32)
              + (jnp.dot(tril, g_mid, preferred_element_type=F32)
                 + jnp.dot(tril, g_lo, preferred_element_type=F32)))
        gcum.append(gc)
        gcum_t.append(gc.T)
        beta.append(beta_all[ci * c:(ci + 1) * c])

    keys = [(ci, h) for ci in range(nch) for h in range(HEADS)]
    gc_col = {k: gcum[k[0]][:, k[1]:k[1] + 1] for k in keys}
    beta_col = {k: beta[k[0]][:, HEADS + k[1]:HEADS + k[1] + 1] for k in keys}
    gam = {k: jnp.where(incl, jnp.exp(jnp.where(incl, gc_col[k] - gcum_t[k[0]][k[1]:k[1] + 1, :], 0.0)), 0.0)
           for k in keys}
    kq = {k: _mm_nt(jnp.concatenate([kn[k], qn[k]], axis=0), kn[k]) for k in keys}
    a = {k: jnp.where(strict, beta_col[k] * kq[k][:c] * gam[k], 0.0) for k in keys}
    qkg = {k: kq[k][c:] * gam[k] for k in keys}
    tinv = dict(zip(keys, _tri_inverse_many([a[k] for k in keys], eye, blk_masks)))
    eg = {k: jnp.exp(gc_col[k]) for k in keys}
    sol = {k: _mm(tinv[k], jnp.concatenate([beta_col[k] * vv[k], (beta_col[k] * eg[k]) * kn[k]], axis=-1))
           for k in keys}
    g_last = {k: gc_col[k][c - 1:c, :] for k in keys}
    kd = {k: kn[k] * jnp.exp(g_last[k] - gc_col[k]) for k in keys}

    gnw = gnw_ref[...]
    s = [s_scr[h] for h in range(HEADS)]
    for ci in range(nch):
        rows = slice(ci * c, (ci + 1) * c)
        ks = [(ci, h) for h in range(HEADS)]
        ws = [_mm(jnp.concatenate([sol[k][:, HEAD_V:], qn[k]], axis=0), s[k[1]]) for k in ks]
        v_new = [sol[k][:, :HEAD_V] - w[:c] for k, w in zip(ks, ws)]
        o = [eg[k] * w[c:] + _mm(qkg[k], vn) for k, w, vn in zip(ks, ws, v_new)]
        s = [jnp.exp(g_last[k]) * s[k[1]] + _mm_tn(kd[k], vn) for k, vn in zip(ks, v_new)]
        for h in range(HEADS):
            on = o[h] * lax.rsqrt(jnp.mean(o[h] * o[h], axis=-1, keepdims=True) + RMS_EPS) * gnw
            z = p_ref[0, rows, C_ZG + h * HEAD_V:C_ZG + (h + 1) * HEAD_V]
            gdn_ref[0, rows, h * HEAD_V:(h + 1) * HEAD_V] = (on * _silu(z)).astype(gdn_ref.dtype)
    for h in range(HEADS):
        s_scr[h] = s[h]

    glu = p_ref[0, :, C_GA:C_GA + CONF_WIDTH] * jax.nn.sigmoid(p_ref[0, :, C_GB:C_GB + CONF_WIDTH])
    cbuf[CHIST_ROWS:CHIST_ROWS + t, :] = glu
    cbase = CHIST_ROWS - (CONF_KERNEL - 1)
    span = t + CHIST_ROWS - SUBLANES
    for sft in range(1, SUBLANES):
        shbuf[sft - 1] = cbuf[sft:sft + span, :]
    cv_blocks = []
    for cb in range(CONF_WIDTH // LANES):
        sl = slice(cb * LANES, (cb + 1) * LANES)
        acc = None
        for k in range(CONF_KERNEL):
            m, sft = divmod(cbase + k, SUBLANES)
            if sft == 0:
                xs = cbuf[SUBLANES * m:SUBLANES * m + t, sl]
            else:
                xs = shbuf[sft - 1, SUBLANES * m:SUBLANES * m + t, sl]
            term = dww_ref[k:k + 1, sl] * xs
            acc = term if acc is None else acc + term
        cv_blocks.append(acc + dwb_ref[:, sl])
    cv = jnp.concatenate(cv_blocks, axis=-1)
    mu = jnp.mean(cv, axis=-1, keepdims=True)
    cc = cv - mu
    cn = cc * lax.rsqrt(jnp.mean(cc * cc, axis=-1, keepdims=True) + LN_EPS)
    cn = cn * lnw_ref[...] + lnb_ref[...]
    zc = p_ref[0, :, C_ZC:C_ZC + CONF_WIDTH]
    conf_ref[0] = (_silu(cn) * _silu(zc)).astype(conf_ref.dtype)
    new_chist = cbuf[t:t + CHIST_ROWS, :]
    cbuf[0:CHIST_ROWS, :] = new_chist

    @pl.when(step == last)
    def _():
        s_out_ref[0] = s_scr[...]
        g_out_ref[0] = xbuf[0:GHIST_ROWS, :]
        c_out_ref[0] = cbuf[0:CHIST_ROWS, :]


def _seq_call(p, ab, s0, g0, c0, wts, t, c):
    bsz, seq, _ = p.shape
    cw, alog, dtb, gnw, dww, dwb, lnw, lnb = wts
    full = lambda shape: pl.BlockSpec(shape, lambda b, s: (0,) * len(shape))
    return pl.pallas_call(
        functools.partial(_seq_kernel, t=t, c=c),
        grid=(bsz, seq // t),
        in_specs=[pl.BlockSpec((1, t, P_COLS), lambda b, s: (b, s, 0)),
                  pl.BlockSpec((1, t, LANES), lambda b, s: (b, s, 0)),
                  full((HEADS, HEAD_K, HEAD_V)),
                  full((GHIST_ROWS, QKV_WIDTH)),
                  full((CHIST_ROWS, CONF_WIDTH)),
                  full(cw.shape), full(alog.shape), full(dtb.shape), full(gnw.shape),
                  full(dww.shape), full(dwb.shape), full(lnw.shape), full(lnb.shape)],
        out_specs=[pl.BlockSpec((1, t, GDN_WIDTH), lambda b, s: (b, s, 0)),
                   pl.BlockSpec((1, t, CONF_WIDTH), lambda b, s: (b, s, 0)),
                   pl.BlockSpec((1, HEADS, HEAD_K, HEAD_V), lambda b, s: (b, 0, 0, 0)),
                   pl.BlockSpec((1, GHIST_ROWS, QKV_WIDTH), lambda b, s: (b, 0, 0)),
                   pl.BlockSpec((1, CHIST_ROWS, CONF_WIDTH), lambda b, s: (b, 0, 0))],
        out_shape=[jax.ShapeDtypeStruct((bsz, seq, GDN_WIDTH), BF16),
                   jax.ShapeDtypeStruct((bsz, seq, CONF_WIDTH), BF16),
                   jax.ShapeDtypeStruct((bsz, HEADS, HEAD_K, HEAD_V), F32),
                   jax.ShapeDtypeStruct((bsz, GHIST_ROWS, QKV_WIDTH), F32),
                   jax.ShapeDtypeStruct((bsz, CHIST_ROWS, CONF_WIDTH), F32)],
        scratch_shapes=[pltpu.VMEM((HEADS, HEAD_K, HEAD_V), F32),
                        pltpu.VMEM((t + GHIST_ROWS, QKV_WIDTH), F32),
                        pltpu.VMEM((t + CHIST_ROWS, CONF_WIDTH), F32),
                        pltpu.VMEM((SUBLANES - 1, t + CHIST_ROWS - SUBLANES, CONF_WIDTH), F32)],
        compiler_params=pltpu.CompilerParams(dimension_semantics=("parallel", "arbitrary"),
                                             vmem_limit_bytes=VMEM_LIMIT),
        name=f"seq_t{t}",
    )(p, ab, s0, g0, c0, cw, alog, dtb, gnw, dww, dwb, lnw, lnb)


def _step_kernel(p_ref, ab_ref, s_ref, gh_ref, ch_ref, cw_ref, alog_ref, dtb_ref, gnw_ref,
                 dww_ref, dwb_ref, lnw_ref, lnb_ref,
                 gdn_ref, conf_ref, s_out_ref, g_out_ref, c_out_ref, o_scr, *, nb):
    nh = SHORT_CONV - 1
    nc = CONF_KERNEL - 1
    x = p_ref[:, C_QKV:C_QKV + QKV_WIDTH]
    conv = cw_ref[0:1, :] * gh_ref[0]
    for k in range(1, nh):
        conv = conv + cw_ref[k:k + 1, :] * gh_ref[k]
        g_out_ref[k - 1] = gh_ref[k]
    conv = conv + cw_ref[nh:nh + 1, :] * x
    g_out_ref[nh - 1] = x
    qkv = _silu(conv)

    glu = p_ref[:, C_GA:C_GA + CONF_WIDTH] * jax.nn.sigmoid(p_ref[:, C_GB:C_GB + CONF_WIDTH])
    cv = dww_ref[0:1, :] * ch_ref[0]
    for k in range(1, nc):
        cv = cv + dww_ref[k:k + 1, :] * ch_ref[k]
        c_out_ref[k - 1] = ch_ref[k]
    cv = cv + dww_ref[nc:nc + 1, :] * glu + dwb_ref[...]
    c_out_ref[nc - 1] = glu

    ab = ab_ref[...]
    g_all = -jnp.exp(alog_ref[...]) * _softplus(ab + dtb_ref[...])
    eg_all = jnp.exp(g_all)
    beta_all = jax.nn.sigmoid(ab)
    row = lax.broadcasted_iota(jnp.int32, (nb, HEAD_K), 0)

    for h in range(HEADS):
        qh = qkv[:, h * HEAD_K:(h + 1) * HEAD_K]
        kh = qkv[:, GDN_WIDTH + h * HEAD_K:GDN_WIDTH + (h + 1) * HEAD_K]
        vh = qkv[:, 2 * GDN_WIDTH + h * HEAD_V:2 * GDN_WIDTH + (h + 1) * HEAD_V]
        qn = qh * lax.rsqrt(jnp.sum(qh * qh, axis=-1, keepdims=True) + RMS_EPS) * (HEAD_K ** -0.5)
        kn = kh * lax.rsqrt(jnp.sum(kh * kh, axis=-1, keepdims=True) + RMS_EPS)
        qk = jnp.sum(qn * kn, axis=-1, keepdims=True)
        kq = jnp.concatenate([kn, qn], axis=0).astype(BF16)
        prod = [jnp.dot(kq, s_ref[b, h].astype(BF16), preferred_element_type=F32) for b in range(nb)]
        ks = prod[0][:nb]
        qs = prod[0][nb:]
        for b in range(1, nb):
            ks = jnp.where(row == b, prod[b][:nb], ks)
            qs = jnp.where(row == b, prod[b][nb:], qs)
        eg = eg_all[:, h:h + 1]
        bt = beta_all[:, HEADS + h:HEADS + h + 1]
        v_new = bt * vh - (bt * eg) * ks
        o_scr[:, h * HEAD_V:(h + 1) * HEAD_V] = eg * qs + qk * v_new
        for b in range(nb):
            outer = _mm_tn(jnp.where(row == b, kn, 0.0), v_new)
            s_out_ref[b, h] = eg_all[b:b + 1, h:h + 1] * s_ref[b, h] + outer

    gnw = gnw_ref[...]
    for h in range(HEADS):
        o = o_scr[:, h * HEAD_V:(h + 1) * HEAD_V]
        on = o * lax.rsqrt(jnp.mean(o * o, axis=-1, keepdims=True) + RMS_EPS) * gnw
        z = p_ref[:, C_ZG + h * HEAD_V:C_ZG + (h + 1) * HEAD_V]
        gdn_ref[:, h * HEAD_V:(h + 1) * HEAD_V] = (on * _silu(z)).astype(gdn_ref.dtype)

    mu = jnp.mean(cv, axis=-1, keepdims=True)
    cc = cv - mu
    cn = cc * lax.rsqrt(jnp.mean(cc * cc, axis=-1, keepdims=True) + LN_EPS)
    cn = cn * lnw_ref[...] + lnb_ref[...]
    conf_ref[...] = (_silu(cn) * _silu(p_ref[:, C_ZC:C_ZC + CONF_WIDTH])).astype(conf_ref.dtype)


def _step_call(p, ab, s, gh, ch, wts, nb, out_dtype):
    bsz = p.shape[0]
    cw, alog, dtb, gnw, dww, dwb, lnw, lnb = wts
    full = lambda shape: pl.BlockSpec(shape, lambda i: (0,) * len(shape))
    nh = SHORT_CONV - 1
    nc = CONF_KERNEL - 1
    return pl.pallas_call(
        functools.partial(_step_kernel, nb=nb),
        grid=(bsz // nb,),
        in_specs=[pl.BlockSpec((nb, P_COLS), lambda i: (i, 0)),
                  pl.BlockSpec((nb, LANES), lambda i: (i, 0)),
                  pl.BlockSpec((nb, HEADS, HEAD_K, HEAD_V), lambda i: (i, 0, 0, 0)),
                  pl.BlockSpec((nh, nb, QKV_WIDTH), lambda i: (0, i, 0)),
                  pl.BlockSpec((nc, nb, CONF_WIDTH), lambda i: (0, i, 0)),
                  full(cw.shape), full(alog.shape), full(dtb.shape), full(gnw.shape),
                  full(dww.shape), full(dwb.shape), full(lnw.shape), full(lnb.shape)],
        out_specs=[pl.BlockSpec((nb, GDN_WIDTH), lambda i: (i, 0)),
                   pl.BlockSpec((nb, CONF_WIDTH), lambda i: (i, 0)),
                   pl.BlockSpec((nb, HEADS, HEAD_K, HEAD_V), lambda i: (i, 0, 0, 0)),
                   pl.BlockSpec((nh, nb, QKV_WIDTH), lambda i: (0, i, 0)),
                   pl.BlockSpec((nc, nb, CONF_WIDTH), lambda i: (0, i, 0))],
        out_shape=[jax.ShapeDtypeStruct((bsz, GDN_WIDTH), BF16),
                   jax.ShapeDtypeStruct((bsz, CONF_WIDTH), BF16),
                   jax.ShapeDtypeStruct(s.shape, out_dtype),
                   jax.ShapeDtypeStruct(gh.shape, out_dtype),
                   jax.ShapeDtypeStruct(ch.shape, out_dtype)],
        scratch_shapes=[pltpu.VMEM((nb, GDN_WIDTH), F32)],
        compiler_params=pltpu.CompilerParams(dimension_semantics=("parallel",),
                                             vmem_limit_bytes=VMEM_LIMIT),
        name="decode_step",
    )(p, ab, s, gh, ch, cw, alog, dtb, gnw, dww, dwb, lnw, lnb)


def _out_kernel(gdn_ref, conf_ref, w_ref, x_ref, fw_ref, o_ref):
    y = (jnp.dot(gdn_ref[...], w_ref[0:GDN_WIDTH, :], preferred_element_type=F32)
         + jnp.dot(conf_ref[...], w_ref[GDN_WIDTH:, :], preferred_element_type=F32))
    hres = x_ref[...] + y
    o_ref[...] = hres * lax.rsqrt(jnp.mean(hres * hres, axis=-1, keepdims=True) + RMS_EPS) * fw_ref[...]


def _out_call(gdn, conf, w, x, fw, tm):
    m = x.shape[0]
    return pl.pallas_call(
        _out_kernel,
        grid=(m // tm,),
        in_specs=[pl.BlockSpec((tm, GDN_WIDTH), lambda i: (i, 0)),
                  pl.BlockSpec((tm, CONF_WIDTH), lambda i: (i, 0)),
                  pl.BlockSpec(w.shape, lambda i: (0, 0)),
                  pl.BlockSpec((tm, D_MODEL), lambda i: (i, 0)),
                  pl.BlockSpec((1, D_MODEL), lambda i: (0, 0))],
        out_specs=pl.BlockSpec((tm, D_MODEL), lambda i: (i, 0)),
        out_shape=jax.ShapeDtypeStruct((m, D_MODEL), F32),
        compiler_params=pltpu.CompilerParams(dimension_semantics=("parallel",),
                                             vmem_limit_bytes=VMEM_LIMIT),
        name="out_proj",
    )(gdn, conf, w, x, fw)


def _pad_lanes(v):
    return jnp.pad(v.astype(F32), (0, LANES - v.shape[0]))[None, :]


def kernel(x_prompt, x_sample, state_gdn_S, state_gdn_conv, state_conf_conv, meta_tokens, norm_w, w_in,
           gdn_conv_w, gdn_A_log, gdn_dt_bias, gdn_norm_w, conf_dw_w, conf_dw_b, conf_ln_w, conf_ln_b,
           w_out, final_norm_w):
    bsz, seq, _ = x_prompt.shape
    dec = x_sample.shape[0]
    l = 0
    wt = jnp.swapaxes(w_in[l], 0, 1)
    w_o = w_out[l].astype(BF16)
    nw = norm_w[l][None, :]
    fw = final_norm_w[None, :]
    wts = (gdn_conv_w[l], _pad_lanes(gdn_A_log[l]), _pad_lanes(gdn_dt_bias[l]), gdn_norm_w[l][None, :],
           conf_dw_w[l], conf_dw_b[l][None, :], conf_ln_w[l][None, :], conf_ln_b[l][None, :])

    xp = x_prompt.reshape(bsz * seq, D_MODEL)
    xs = x_sample.reshape(dec, D_MODEL)
    x_small = jnp.concatenate([xs, meta_tokens.astype(xs.dtype)], axis=0)
    n_small = dec + N_META

    xn_p = _norm_cast(xp, nw, 512)
    xn_s = _norm_cast(x_small, nw, n_small)
    n_main = P_COLS // IN_TILE
    p_prompt = _in_proj(xn_p, wt, _main_row_start, n_main, IN_TILE, 1024, "in_proj")
    ab_prompt = _in_proj(xn_p, wt, _gate_row_start, 1, LANES, 2048, "in_proj_gates")
    p_small = _in_proj(xn_s, wt, _main_row_start, n_main, IN_TILE, n_small, "in_proj_small")
    ab_small = _in_proj(xn_s, wt, _gate_row_start, 1, LANES, n_small, "in_proj_gates_small")

    zs = jnp.zeros((HEADS, HEAD_K, HEAD_V), F32)
    zg = jnp.zeros((GHIST_ROWS, QKV_WIDTH), F32)
    zc = jnp.zeros((CHIST_ROWS, CONF_WIDTH), F32)
    _, _, s_m, g_m, c_m = _seq_call(p_small[dec:][None], ab_small[dec:][None], zs, zg, zc, wts, N_META, N_META)

    gdn_p, conf_p, s_p, g_p, c_p = _seq_call(p_prompt.reshape(bsz, seq, P_COLS),
                                             ab_prompt.reshape(bsz, seq, LANES),
                                             s_m[0], g_m[0], c_m[0], wts, 2 * GDN_CHUNK, GDN_CHUNK)
    y_prompt = _out_call(gdn_p.reshape(bsz * seq, GDN_WIDTH), conf_p.reshape(bsz * seq, CONF_WIDTH),
                         w_o, xp, fw, 512).reshape(bsz, seq, D_MODEL)

    sdt = state_gdn_S.dtype
    gh = jnp.swapaxes(state_gdn_conv[l], 0, 1)
    ch = jnp.swapaxes(state_conf_conv[l], 0, 1)
    gdn_s, conf_s, s_s, g_s, c_s = _step_call(p_small[:dec], ab_small[:dec], state_gdn_S[l], gh, ch,
                                              wts, SUBLANES, sdt)
    y_sample = _out_call(gdn_s, conf_s, w_o, xs, fw, dec).reshape(dec, 1, D_MODEL)

    nh = SHORT_CONV - 1
    nc = CONF_KERNEL - 1
    return (y_prompt, y_sample,
            s_p.astype(sdt)[None],
            g_p[:, GHIST_ROWS - nh:, :].astype(state_gdn_conv.dtype)[None],
            c_p[:, CHIST_ROWS - nc:, :].astype(state_conf_conv.dtype)[None],
            s_s[None], jnp.swapaxes(g_s, 0, 1)[None], jnp.swapaxes(c_s, 0, 1)[None])
```

```python
import functools

import jax
import jax.numpy as jnp
from jax import lax
from jax.experimental import pallas as pl
from jax.experimental.pallas import tpu as pltpu

F32 = jnp.float32
BF16 = jnp.bfloat16

D_MODEL = 2048
N_META = 16
HEADS = 8
HEAD_K = 128
HEAD_V = 128
GDN_WIDTH = HEADS * HEAD_V
CONF_WIDTH = 1024
QKV_WIDTH = 3 * GDN_WIDTH
SHORT_CONV = 4
CONF_KERNEL = 31
GDN_CHUNK = 64
RMS_EPS = 1e-6
LN_EPS = 1e-5

LANES = 128
SUBLANES = 8

C_QKV = 0
C_ZG = QKV_WIDTH
C_GA = C_ZG + GDN_WIDTH
C_GB = C_GA + CONF_WIDTH
C_ZC = C_GB + CONF_WIDTH
P_COLS = C_ZC + CONF_WIDTH
N_GATES = 2 * HEADS
IN_TILE = 1024

GHIST_ROWS = SUBLANES
CHIST_ROWS = 32

VMEM_LIMIT = 56 * 1024 * 1024


def _silu(x):
    return x * jax.nn.sigmoid(x)


def _softplus(x):
    return jnp.maximum(x, 0.0) + jnp.log1p(jnp.exp(-jnp.abs(x)))


def _mm(a, b):
    return jnp.dot(a.astype(BF16), b.astype(BF16), preferred_element_type=F32)


def _mm_nt(a, b):
    return lax.dot_general(a.astype(BF16), b.astype(BF16), (((1,), (1,)), ((), ())),
                           preferred_element_type=F32)


def _mm_tn(a, b):
    return lax.dot_general(a.astype(BF16), b.astype(BF16), (((0,), (0,)), ((), ())),
                           preferred_element_type=F32)


def _norm_cast_kernel(x_ref, w_ref, o_ref):
    x = x_ref[...]
    y = x * lax.rsqrt(jnp.mean(x * x, axis=-1, keepdims=True) + RMS_EPS)
    o_ref[...] = (y * w_ref[...]).astype(BF16)


def _norm_cast(x, w, tm):
    m = x.shape[0]
    return pl.pallas_call(
        _norm_cast_kernel,
        grid=(m // tm,),
        in_specs=[pl.BlockSpec((tm, D_MODEL), lambda i: (i, 0)),
                  pl.BlockSpec((1, D_MODEL), lambda i: (0, 0))],
        out_specs=pl.BlockSpec((tm, D_MODEL), lambda i: (i, 0)),
        out_shape=jax.ShapeDtypeStruct((m, D_MODEL), BF16),
        compiler_params=pltpu.CompilerParams(dimension_semantics=("parallel",),
                                             vmem_limit_bytes=VMEM_LIMIT),
        name="norm_cast",
    )(x, w)


def _in_proj_kernel(x_ref, xs_ref, wt_ref, o_ref, os_ref, wb_scr):
    @pl.when(pl.program_id(1) == 0)
    def _():
        wb_scr[...] = wt_ref[...].astype(BF16)
        os_ref[...] = lax.dot_general(xs_ref[...], wb_scr[...], (((1,), (1,)), ((), ())),
                                      preferred_element_type=F32)

    o_ref[...] = lax.dot_general(x_ref[...], wb_scr[...], (((1,), (1,)), ((), ())),
                                 preferred_element_type=F32)


def _in_proj(x, xs, wt, row_start, n_tiles, tn, tm, name):
    m, k = x.shape
    ms = xs.shape[0]
    return pl.pallas_call(
        _in_proj_kernel,
        grid=(n_tiles, m // tm),
        in_specs=[pl.BlockSpec((tm, k), lambda j, i: (i, 0)),
                  pl.BlockSpec((ms, k), lambda j, i: (0, 0)),
                  pl.BlockSpec((pl.Element(tn), pl.Element(k)), lambda j, i: (row_start(j), 0))],
        out_specs=[pl.BlockSpec((tm, tn), lambda j, i: (i, j)),
                   pl.BlockSpec((ms, tn), lambda j, i: (0, j))],
        out_shape=[jax.ShapeDtypeStruct((m, n_tiles * tn), F32),
                   jax.ShapeDtypeStruct((ms, n_tiles * tn), F32)],
        scratch_shapes=[pltpu.VMEM((tn, k), BF16)],
        compiler_params=pltpu.CompilerParams(dimension_semantics=("arbitrary", "arbitrary"),
                                             vmem_limit_bytes=VMEM_LIMIT),
        name=name,
    )(x, xs, wt)


def _main_row_start(j):
    return (j * (IN_TILE // N_GATES) + jnp.where(j >= QKV_WIDTH // IN_TILE, 1, 0)) * N_GATES


def _gate_row_start(j):
    return (QKV_WIDTH // N_GATES + 0 * j) * N_GATES


def _tri_inverse_many(a_list, eye, blk_masks):
    d0 = [jnp.where(blk_masks[0], a, 0.0) for a in a_list]
    x = [eye - d for d in d0]
    d2 = [_mm(d, d) for d in d0]
    x = [xi + _mm(xi, di) for xi, di in zip(x, d2)]
    d4 = [_mm(d, d) for d in d2]
    x = [xi + _mm(xi, di) for xi, di in zip(x, d4)]
    for lvl in range(1, len(blk_masks)):
        off_mask = blk_masks[lvl] & jnp.logical_not(blk_masks[lvl - 1])
        off = [jnp.where(off_mask, a, 0.0) for a in a_list]
        y = [_mm(xi, oi) for xi, oi in zip(x, off)]
        x = [xi - _mm(yi, xi) for xi, yi in zip(x, y)]
    return x


def _seq_kernel(p_ref, ab_ref, s0_ref, g0_ref, c0_ref, cw_ref, alog_ref, dtb_ref, gnw_ref,
                dww_ref, dwb_ref, lnw_ref, lnb_ref,
                gdn_ref, conf_ref, s_out_ref, g_out_ref, c_out_ref,
                s_scr, xbuf, cbuf, shbuf, *, t, c):
    step = pl.program_id(1)
    last = pl.num_programs(1) - 1
    nch = t // c

    @pl.when(step == 0)
    def _():
        s_scr[...] = s0_ref[...]
        xbuf[0:GHIST_ROWS, :] = g0_ref[...]
        cbuf[0:CHIST_ROWS, :] = c0_ref[...]

    xbuf[GHIST_ROWS:GHIST_ROWS + t, :] = p_ref[0, :, C_QKV:C_QKV + QKV_WIDTH]
    base = GHIST_ROWS - (SHORT_CONV - 1)

    def conv_block(lo):
        sl = slice(lo, lo + LANES)
        acc = cw_ref[0:1, sl] * xbuf[base:base + t, sl]
        for k in range(1, SHORT_CONV):
            acc = acc + cw_ref[k:k + 1, sl] * xbuf[base + k:base + k + t, sl]
        return _silu(acc)

    qn, kn, vv = {}, {}, {}
    for h in range(HEADS):
        qf = conv_block(h * HEAD_K)
        kf = conv_block(GDN_WIDTH + h * HEAD_K)
        vf = conv_block(2 * GDN_WIDTH + h * HEAD_V)
        qf = qf * lax.rsqrt(jnp.sum(qf * qf, axis=-1, keepdims=True) + RMS_EPS) * (HEAD_K ** -0.5)
        kf = kf * lax.rsqrt(jnp.sum(kf * kf, axis=-1, keepdims=True) + RMS_EPS)
        for ci in range(nch):
            rows = slice(ci * c, (ci + 1) * c)
            qn[ci, h], kn[ci, h], vv[ci, h] = qf[rows], kf[rows], vf[rows]
    new_hist = xbuf[t:t + GHIST_ROWS, :]
    xbuf[0:GHIST_ROWS, :] = new_hist

    ab = ab_ref[0]
    g_all = -jnp.exp(alog_ref[...]) * _softplus(ab + dtb_ref[...])
    beta_all = jax.nn.sigmoid(ab)

    r = lax.broadcasted_iota(jnp.int32, (c, c), 0)
    q = lax.broadcasted_iota(jnp.int32, (c, c), 1)
    incl = r >= q
    strict = r > q
    eye = (r == q).astype(F32)
    tril = incl.astype(BF16)
    blk_masks = []
    b = 8
    while b <= c:
        sh = b.bit_length() - 1
        blk_masks.append((r >> sh) == (q >> sh))
        b *= 2

    gcum, gcum_t, beta = [], [], []
    for ci in range(nch):
        g = g_all[ci * c:(ci + 1) * c]
        g_hi = g.astype(BF16)
        r1 = g - g_hi.astype(F32)
        g_mid = r1.astype(BF16)
        g_lo = (r1 - g_mid.astype(F32)).astype(BF16)
        gc = (jnp.dot(tril, g_hi, preferred_element_type=F32)
              + (jnp.dot(tril, g_mid, preferred_element_type=F32)
                 + jnp.dot(tril, g_lo, preferred_element_type=F32)))
        gcum.append(gc)
        gcum_t.append(gc.T)
        beta.append(beta_all[ci * c:(ci + 1) * c])

    keys = [(ci, h) for ci in range(nch) for h in range(HEADS)]
    gc_col = {k: gcum[k[0]][:, k[1]:k[1] + 1] for k in keys}
    beta_col = {k: beta[k[0]][:, HEADS + k[1]:HEADS + k[1] + 1] for k in keys}
    gam = {k: jnp.where(incl, jnp.exp(jnp.where(incl, gc_col[k] - gcum_t[k[0]][k[1]:k[1] + 1, :], 0.0)), 0.0)
           for k in keys}
    kq = {k: _mm_nt(jnp.concatenate([kn[k], qn[k]], axis=0), kn[k]) for k in keys}
    a = {k: jnp.where(strict, beta_col[k] * kq[k][:c] * gam[k], 0.0) for k in keys}
    qkg = {k: kq[k][c:] * gam[k] for k in keys}
    tinv = dict(zip(keys, _tri_inverse_many([a[k] for k in keys], eye, blk_masks)))
    eg = {k: jnp.exp(gc_col[k]) for k in keys}
    sol = {k: _mm(tinv[k], jnp.concatenate([beta_col[k] * vv[k], (beta_col[k] * eg[k]) * kn[k]], axis=-1))
           for k in keys}
    g_last = {k: gc_col[k][c - 1:c, :] for k in keys}
    kd = {k: kn[k] * jnp.exp(g_last[k] - gc_col[k]) for k in keys}

    gnw = gnw_ref[...]
    s = [s_scr[h] for h in range(HEADS)]
    for ci in range(nch):
        rows = slice(ci * c, (ci + 1) * c)
        ks = [(ci, h) for h in range(HEADS)]
        ws = [_mm(jnp.concatenate([sol[k][:, HEAD_V:], qn[k]], axis=0), s[k[1]]) for k in ks]
        v_new = [sol[k][:, :HEAD_V] - w[:c] for k, w in zip(ks, ws)]
        o = [eg[k] * w[c:] + _mm(qkg[k], vn) for k, w, vn in zip(ks, ws, v_new)]
        s = [jnp.exp(g_last[k]) * s[k[1]] + _mm_tn(kd[k], vn) for k, vn in zip(ks, v_new)]
        for h in range(HEADS):
            on = o[h] * lax.rsqrt(jnp.mean(o[h] * o[h], axis=-1, keepdims=True) + RMS_EPS) * gnw
            z = p_ref[0, rows, C_ZG + h * HEAD_V:C_ZG + (h + 1) * HEAD_V]
            gdn_ref[0, rows, h * HEAD_V:(h + 1) * HEAD_V] = (on * _silu(z)).astype(gdn_ref.dtype)
    for h in range(HEADS):
        s_scr[h] = s[h]

    glu = p_ref[0, :, C_GA:C_GA + CONF_WIDTH] * jax.nn.sigmoid(p_ref[0, :, C_GB:C_GB + CONF_WIDTH])
    cbuf[CHIST_ROWS:CHIST_ROWS + t, :] = glu
    cbase = CHIST_ROWS - (CONF_KERNEL - 1)
    span = t + CHIST_ROWS - SUBLANES
    for sft in range(1, SUBLANES):
        shbuf[sft - 1] = cbuf[sft:sft + span, :]
    cv_blocks = []
    for cb in range(CONF_WIDTH // LANES):
        sl = slice(cb * LANES, (cb + 1) * LANES)
        acc = None
        for k in range(CONF_KERNEL):
            m, sft = divmod(cbase + k, SUBLANES)
            if sft == 0:
                xs = cbuf[SUBLANES * m:SUBLANES * m + t, sl]
            else:
                xs = shbuf[sft - 1, SUBLANES * m:SUBLANES * m + t, sl]
            term = dww_ref[k:k + 1, sl] * xs
            acc = term if acc is None else acc + term
        cv_blocks.append(acc + dwb_ref[:, sl])
    cv = jnp.concatenate(cv_blocks, axis=-1)
    mu = jnp.mean(cv, axis=-1, keepdims=True)
    cc = cv - mu
    cn = cc * lax.rsqrt(jnp.mean(cc * cc, axis=-1, keepdims=True) + LN_EPS)
    cn = cn * lnw_ref[...] + lnb_ref[...]
    zc = p_ref[0, :, C_ZC:C_ZC + CONF_WIDTH]
    conf_ref[0] = (_silu(cn) * _silu(zc)).astype(conf_ref.dtype)
    new_chist = cbuf[t:t + CHIST_ROWS, :]
    cbuf[0:CHIST_ROWS, :] = new_chist

    @pl.when(step == last)
    def _():
        s_out_ref[0] = s_scr[...]
        g_out_ref[0] = xbuf[0:GHIST_ROWS, :]
        c_out_ref[0] = cbuf[0:CHIST_ROWS, :]


def _seq_call(p, ab, s0, g0, c0, wts, t, c):
    bsz, seq, _ = p.shape
    cw, alog, dtb, gnw, dww, dwb, lnw, lnb = wts
    full = lambda shape: pl.BlockSpec(shape, lambda b, s: (0,) * len(shape))
    return pl.pallas_call(
        functools.partial(_seq_kernel, t=t, c=c),
        grid=(bsz, seq // t),
        in_specs=[pl.BlockSpec((1, t, P_COLS), lambda b, s: (b, s, 0)),
                  pl.BlockSpec((1, t, LANES), lambda b, s: (b, s, 0)),
                  full((HEADS, HEAD_K, HEAD_V)),
                  full((GHIST_ROWS, QKV_WIDTH)),
                  full((CHIST_ROWS, CONF_WIDTH)),
                  full(cw.shape), full(alog.shape), full(dtb.shape), full(gnw.shape),
                  full(dww.shape), full(dwb.shape), full(lnw.shape), full(lnb.shape)],
        out_specs=[pl.BlockSpec((1, t, GDN_WIDTH), lambda b, s: (b, s, 0)),
                   pl.BlockSpec((1, t, CONF_WIDTH), lambda b, s: (b, s, 0)),
                   pl.BlockSpec((1, HEADS, HEAD_K, HEAD_V), lambda b, s: (b, 0, 0, 0)),
                   pl.BlockSpec((1, GHIST_ROWS, QKV_WIDTH), lambda b, s: (b, 0, 0)),
                   pl.BlockSpec((1, CHIST_ROWS, CONF_WIDTH), lambda b, s: (b, 0, 0))],
        out_shape=[jax.ShapeDtypeStruct((bsz, seq, GDN_WIDTH), BF16),
                   jax.ShapeDtypeStruct((bsz, seq, CONF_WIDTH), BF16),
                   jax.ShapeDtypeStruct((bsz, HEADS, HEAD_K, HEAD_V), F32),
                   jax.ShapeDtypeStruct((bsz, GHIST_ROWS, QKV_WIDTH), F32),
                   jax.ShapeDtypeStruct((bsz, CHIST_ROWS, CONF_WIDTH), F32)],
        scratch_shapes=[pltpu.VMEM((HEADS, HEAD_K, HEAD_V), F32),
                        pltpu.VMEM((t + GHIST_ROWS, QKV_WIDTH), F32),
                        pltpu.VMEM((t + CHIST_ROWS, CONF_WIDTH), F32),
                        pltpu.VMEM((SUBLANES - 1, t + CHIST_ROWS - SUBLANES, CONF_WIDTH), F32)],
        compiler_params=pltpu.CompilerParams(dimension_semantics=("parallel", "arbitrary"),
                                             vmem_limit_bytes=VMEM_LIMIT),
        name=f"seq_t{t}",
    )(p, ab, s0, g0, c0, cw, alog, dtb, gnw, dww, dwb, lnw, lnb)


def _step_kernel(p_ref, ab_ref, s_ref, gh_ref, ch_ref, cw_ref, alog_ref, dtb_ref, gnw_ref,
                 dww_ref, dwb_ref, lnw_ref, lnb_ref,
                 gdn_ref, conf_ref, s_out_ref, g_out_ref, c_out_ref, o_scr, *, nb):
    nh = SHORT_CONV - 1
    nc = CONF_KERNEL - 1
    x = p_ref[:, C_QKV:C_QKV + QKV_WIDTH]
    conv = cw_ref[0:1, :] * gh_ref[0]
    for k in range(1, nh):
        conv = conv + cw_ref[k:k + 1, :] * gh_ref[k]
        g_out_ref[k - 1] = gh_ref[k]
    conv = conv + cw_ref[nh:nh + 1, :] * x
    g_out_ref[nh - 1] = x
    qkv = _silu(conv)

    glu = p_ref[:, C_GA:C_GA + CONF_WIDTH] * jax.nn.sigmoid(p_ref[:, C_GB:C_GB + CONF_WIDTH])
    cv = dww_ref[0:1, :] * ch_ref[0]
    for k in range(1, nc):
        cv = cv + dww_ref[k:k + 1, :] * ch_ref[k]
        c_out_ref[k - 1] = ch_ref[k]
    cv = cv + dww_ref[nc:nc + 1, :] * glu + dwb_ref[...]
    c_out_ref[nc - 1] = glu

    ab = ab_ref[...]
    g_all = -jnp.exp(alog_ref[...]) * _softplus(ab + dtb_ref[...])
    eg_all = jnp.exp(g_all)
    beta_all = jax.nn.sigmoid(ab)
    row = lax.broadcasted_iota(jnp.int32, (nb, HEAD_K), 0)

    for h in range(HEADS):
        qh = qkv[:, h * HEAD_K:(h + 1) * HEAD_K]
        kh = qkv[:, GDN_WIDTH + h * HEAD_K:GDN_WIDTH + (h + 1) * HEAD_K]
        vh = qkv[:, 2 * GDN_WIDTH + h * HEAD_V:2 * GDN_WIDTH + (h + 1) * HEAD_V]
        qn = qh * lax.rsqrt(jnp.sum(qh * qh, axis=-1, keepdims=True) + RMS_EPS) * (HEAD_K ** -0.5)
        kn = kh * lax.rsqrt(jnp.sum(kh * kh, axis=-1, keepdims=True) + RMS_EPS)
        qk = jnp.sum(qn * kn, axis=-1, keepdims=True)
        kq = jnp.concatenate([kn, qn], axis=0).astype(BF16)
        prod = [jnp.dot(kq, s_ref[b, h].astype(BF16), preferred_element_type=F32) for b in range(nb)]
        ks = prod[0][:nb]
        qs = prod[0][nb:]
        for b in range(1, nb):
            ks = jnp.where(row == b, prod[b][:nb], ks)
            qs = jnp.where(row == b, prod[b][nb:], qs)
        eg = eg_all[:, h:h + 1]
        bt = beta_all[:, HEADS + h:HEADS + h + 1]
        v_new = bt * vh - (bt * eg) * ks
        o_scr[:, h * HEAD_V:(h + 1) * HEAD_V] = eg * qs + qk * v_new
        for b in range(nb):
            outer = _mm_tn(jnp.where(row == b, kn, 0.0), v_new)
            s_out_ref[b, h] = eg_all[b:b + 1, h:h + 1] * s_ref[b, h] + outer

    gnw = gnw_ref[...]
    for h in range(HEADS):
        o = o_scr[:, h * HEAD_V:(h + 1) * HEAD_V]
        on = o * lax.rsqrt(jnp.mean(o * o, axis=-1, keepdims=True) + RMS_EPS) * gnw
        z = p_ref[:, C_ZG + h * HEAD_V:C_ZG + (h + 1) * HEAD_V]
        gdn_ref[:, h * HEAD_V:(h + 1) * HEAD_V] = (on * _silu(z)).astype(gdn_ref.dtype)

    mu = jnp.mean(cv, axis=-1, keepdims=True)
    cc = cv - mu
    cn = cc * lax.rsqrt(jnp.mean(cc * cc, axis=-1, keepdims=True) + LN_EPS)
    cn = cn * lnw_ref[...] + lnb_ref[...]
    conf_ref[...] = (_silu(cn) * _silu(p_ref[:, C_ZC:C_ZC + CONF_WIDTH])).astype(conf_ref.dtype)


def _step_call(p, ab, s, gh, ch, wts, nb, out_dtype):
    bsz = p.shape[0]
    cw, alog, dtb, gnw, dww, dwb, lnw, lnb = wts
    full = lambda shape: pl.BlockSpec(shape, lambda i: (0,) * len(shape))
    nh = SHORT_CONV - 1
    nc = CONF_KERNEL - 1
    return pl.pallas_call(
        functools.partial(_step_kernel, nb=nb),
        grid=(bsz // nb,),
        in_specs=[pl.BlockSpec((nb, P_COLS), lambda i: (i, 0)),
                  pl.BlockSpec((nb, LANES), lambda i: (i, 0)),
                  pl.BlockSpec((nb, HEADS, HEAD_K, HEAD_V), lambda i: (i, 0, 0, 0)),
                  pl.BlockSpec((nh, nb, QKV_WIDTH), lambda i: (0, i, 0)),
                  pl.BlockSpec((nc, nb, CONF_WIDTH), lambda i: (0, i, 0)),
                  full(cw.shape), full(alog.shape), full(dtb.shape), full(gnw.shape),
                  full(dww.shape), full(dwb.shape), full(lnw.shape), full(lnb.shape)],
        out_specs=[pl.BlockSpec((nb, GDN_WIDTH), lambda i: (i, 0)),
                   pl.BlockSpec((nb, CONF_WIDTH), lambda i: (i, 0)),
                   pl.BlockSpec((nb, HEADS, HEAD_K, HEAD_V), lambda i: (i, 0, 0, 0)),
                   pl.BlockSpec((nh, nb, QKV_WIDTH), lambda i: (0, i, 0)),
                   pl.BlockSpec((nc, nb, CONF_WIDTH), lambda i: (0, i, 0))],
        out_shape=[jax.ShapeDtypeStruct((bsz, GDN_WIDTH), BF16),
                   jax.ShapeDtypeStruct((bsz, CONF_WIDTH), BF16),
                   jax.ShapeDtypeStruct(s.shape, out_dtype),
                   jax.ShapeDtypeStruct(gh.shape, out_dtype),
                   jax.ShapeDtypeStruct(ch.shape, out_dtype)],
        scratch_shapes=[pltpu.VMEM((nb, GDN_WIDTH), F32)],
        compiler_params=pltpu.CompilerParams(dimension_semantics=("parallel",),
                                             vmem_limit_bytes=VMEM_LIMIT),
        name="decode_step",
    )(p, ab, s, gh, ch, cw, alog, dtb, gnw, dww, dwb, lnw, lnb)


def _out_kernel(gdn_ref, conf_ref, w_ref, x_ref, fw_ref, o_ref):
    y = (jnp.dot(gdn_ref[...], w_ref[0:GDN_WIDTH, :], preferred_element_type=F32)
         + jnp.dot(conf_ref[...], w_ref[GDN_WIDTH:, :], preferred_element_type=F32))
    hres = x_ref[...] + y
    o_ref[...] = hres * lax.rsqrt(jnp.mean(hres * hres, axis=-1, keepdims=True) + RMS_EPS) * fw_ref[...]


def _out_call(gdn, conf, w, x, fw, tm):
    m = x.shape[0]
    return pl.pallas_call(
        _out_kernel,
        grid=(m // tm,),
        in_specs=[pl.BlockSpec((tm, GDN_WIDTH), lambda i: (i, 0)),
                  pl.BlockSpec((tm, CONF_WIDTH), lambda i: (i, 0)),
                  pl.BlockSpec(w.shape, lambda i: (0, 0)),
                  pl.BlockSpec((tm, D_MODEL), lambda i: (i, 0)),
                  pl.BlockSpec((1, D_MODEL), lambda i: (0, 0))],
        out_specs=pl.BlockSpec((tm, D_MODEL), lambda i: (i, 0)),
        out_shape=jax.ShapeDtypeStruct((m, D_MODEL), F32),
        compiler_params=pltpu.CompilerParams(dimension_semantics=("parallel",),
                                             vmem_limit_bytes=VMEM_LIMIT),
        name="out_proj",
    )(gdn, conf, w, x, fw)


def _pad_lanes(v):
    return jnp.pad(v.astype(F32), (0, LANES - v.shape[0]))[None, :]


def kernel(x_prompt, x_sample, state_gdn_S, state_gdn_conv, state_conf_conv, meta_tokens, norm_w, w_in,
           gdn_conv_w, gdn_A_log, gdn_dt_bias, gdn_norm_w, conf_dw_w, conf_dw_b, conf_ln_w, conf_ln_b,
           w_out, final_norm_w):
    bsz, seq, _ = x_prompt.shape
    dec = x_sample.shape[0]
    l = 0
    wt = jnp.swapaxes(w_in[l], 0, 1)
    w_o = w_out[l].astype(BF16)
    nw = norm_w[l][None, :]
    fw = final_norm_w[None, :]
    wts = (gdn_conv_w[l], _pad_lanes(gdn_A_log[l]), _pad_lanes(gdn_dt_bias[l]), gdn_norm_w[l][None, :],
           conf_dw_w[l], conf_dw_b[l][None, :], conf_ln_w[l][None, :], conf_ln_b[l][None, :])

    xp = x_prompt.reshape(bsz * seq, D_MODEL)
    xs = x_sample.reshape(dec, D_MODEL)
    x_small = jnp.concatenate([xs, meta_tokens.astype(xs.dtype)], axis=0)
    n_small = dec + N_META

    xn_p = _norm_cast(xp, nw, 512)
    xn_s = _norm_cast(x_small, nw, n_small)
    n_main = P_COLS // IN_TILE
    p_prompt, p_small = _in_proj(xn_p, xn_s, wt, _main_row_start, n_main, IN_TILE, 1024, "in_proj")
    ab_prompt, ab_small = _in_proj(xn_p, xn_s, wt, _gate_row_start, 1, LANES, 2048, "in_proj_gates")

    zs = jnp.zeros((HEADS, HEAD_K, HEAD_V), F32)
    zg = jnp.zeros((GHIST_ROWS, QKV_WIDTH), F32)
    zc = jnp.zeros((CHIST_ROWS, CONF_WIDTH), F32)
    _, _, s_m, g_m, c_m = _seq_call(p_small[dec:][None], ab_small[dec:][None], zs, zg, zc, wts, N_META, N_META)

    gdn_p, conf_p, s_p, g_p, c_p = _seq_call(p_prompt.reshape(bsz, seq, P_COLS),
                                             ab_prompt.reshape(bsz, seq, LANES),
                                             s_m[0], g_m[0], c_m[0], wts, 2 * GDN_CHUNK, GDN_CHUNK)
    y_prompt = _out_call(gdn_p.reshape(bsz * seq, GDN_WIDTH), conf_p.reshape(bsz * seq, CONF_WIDTH),
                         w_o, xp, fw, 512).reshape(bsz, seq, D_MODEL)

    sdt = state_gdn_S.dtype
    gh = jnp.swapaxes(state_gdn_conv[l], 0, 1)
    ch = jnp.swapaxes(state_conf_conv[l], 0, 1)
    gdn_s, conf_s, s_s, g_s, c_s = _step_call(p_small[:dec], ab_small[:dec], state_gdn_S[l], gh, ch,
                                              wts, 2 * SUBLANES, sdt)
    y_sample = _out_call(gdn_s, conf_s, w_o, xs, fw, dec).reshape(dec, 1, D_MODEL)

    nh = SHORT_CONV - 1
    nc = CONF_KERNEL - 1
    return (y_prompt, y_sample,
            s_p.astype(sdt)[None],
            g_p[:, GHIST_ROWS - nh:, :].astype(state_gdn_conv.dtype)[None],
            c_p[:, CHIST_ROWS - nc:, :].astype(state_conf_conv.dtype)[None],
            s_s[None], jnp.swapaxes(g_s, 0, 1)[None], jnp.swapaxes(c_s, 0, 1)[None])
```

```python
import functools

import jax
import jax.numpy as jnp
from jax import lax
from jax.experimental import pallas as pl
from jax.experimental.pallas import tpu as pltpu

F32 = jnp.float32
BF16 = jnp.bfloat16

D_MODEL = 2048
N_META = 16
HEADS = 8
HEAD_K = 128
HEAD_V = 128
GDN_WIDTH = HEADS * HEAD_V
CONF_WIDTH = 1024
QKV_WIDTH = 3 * GDN_WIDTH
SHORT_CONV = 4
CONF_KERNEL = 31
GDN_CHUNK = 64
RMS_EPS = 1e-6
LN_EPS = 1e-5

LANES = 128
SUBLANES = 8

C_QKV = 0
C_ZG = QKV_WIDTH
C_GA = C_ZG + GDN_WIDTH
C_GB = C_GA + CONF_WIDTH
C_ZC = C_GB + CONF_WIDTH
P_COLS = C_ZC + CONF_WIDTH
N_GATES = 2 * HEADS
IN_TILE = 1024

GHIST_ROWS = SUBLANES
CHIST_ROWS = 32

VMEM_LIMIT = 56 * 1024 * 1024


def _silu(x):
    return x * jax.nn.sigmoid(x)


def _softplus(x):
    return jnp.maximum(x, 0.0) + jnp.log1p(jnp.exp(-jnp.abs(x)))


def _mm(a, b):
    return jnp.dot(a.astype(BF16), b.astype(BF16), preferred_element_type=F32)


def _mm_nt(a, b):
    return lax.dot_general(a.astype(BF16), b.astype(BF16), (((1,), (1,)), ((), ())),
                           preferred_element_type=F32)


def _mm_tn(a, b):
    return lax.dot_general(a.astype(BF16), b.astype(BF16), (((0,), (0,)), ((), ())),
                           preferred_element_type=F32)


def _norm_cast_kernel(x_ref, w_ref, wg_ref, o_ref, g_ref):
    x = x_ref[...]
    y = x * lax.rsqrt(jnp.mean(x * x, axis=-1, keepdims=True) + RMS_EPS)
    xn = (y * w_ref[...]).astype(BF16)
    o_ref[...] = xn
    g_ref[...] = lax.dot_general(xn, wg_ref[...].astype(BF16), (((1,), (1,)), ((), ())),
                                 preferred_element_type=F32)


def _norm_cast(x, w, wt, tm):
    m = x.shape[0]
    return pl.pallas_call(
        _norm_cast_kernel,
        grid=(m // tm,),
        in_specs=[pl.BlockSpec((tm, D_MODEL), lambda i: (i, 0)),
                  pl.BlockSpec((1, D_MODEL), lambda i: (0, 0)),
                  pl.BlockSpec((pl.Element(LANES), pl.Element(D_MODEL)), lambda i: (QKV_WIDTH, 0))],
        out_specs=[pl.BlockSpec((tm, D_MODEL), lambda i: (i, 0)),
                   pl.BlockSpec((tm, LANES), lambda i: (i, 0))],
        out_shape=[jax.ShapeDtypeStruct((m, D_MODEL), BF16),
                   jax.ShapeDtypeStruct((m, LANES), F32)],
        compiler_params=pltpu.CompilerParams(dimension_semantics=("parallel",),
                                             vmem_limit_bytes=VMEM_LIMIT),
        name="norm_cast",
    )(x, w, wt)


def _in_proj_kernel(x_ref, xs_ref, wt_ref, o_ref, os_ref, wb_scr):
    @pl.when(pl.program_id(1) == 0)
    def _():
        wb_scr[...] = wt_ref[...].astype(BF16)
        os_ref[...] = lax.dot_general(xs_ref[...], wb_scr[...], (((1,), (1,)), ((), ())),
                                      preferred_element_type=F32)

    o_ref[...] = lax.dot_general(x_ref[...], wb_scr[...], (((1,), (1,)), ((), ())),
                                 preferred_element_type=F32)


def _in_proj(x, xs, wt, row_start, n_tiles, tn, tm, name):
    m, k = x.shape
    ms = xs.shape[0]
    return pl.pallas_call(
        _in_proj_kernel,
        grid=(n_tiles, m // tm),
        in_specs=[pl.BlockSpec((tm, k), lambda j, i: (i, 0)),
                  pl.BlockSpec((ms, k), lambda j, i: (0, 0)),
                  pl.BlockSpec((pl.Element(tn), pl.Element(k)), lambda j, i: (row_start(j), 0))],
        out_specs=[pl.BlockSpec((tm, tn), lambda j, i: (i, j)),
                   pl.BlockSpec((ms, tn), lambda j, i: (0, j))],
        out_shape=[jax.ShapeDtypeStruct((m, n_tiles * tn), F32),
                   jax.ShapeDtypeStruct((ms, n_tiles * tn), F32)],
        scratch_shapes=[pltpu.VMEM((tn, k), BF16)],
        compiler_params=pltpu.CompilerParams(dimension_semantics=("arbitrary", "arbitrary"),
                                             vmem_limit_bytes=VMEM_LIMIT),
        name=name,
    )(x, xs, wt)


def _main_row_start(j):
    return (j * (IN_TILE // N_GATES) + jnp.where(j >= QKV_WIDTH // IN_TILE, 1, 0)) * N_GATES


def _tri_inverse_many(a_list, eye, blk_masks):
    d0 = [jnp.where(blk_masks[0], a, 0.0) for a in a_list]
    x = [eye - d for d in d0]
    d2 = [_mm(d, d) for d in d0]
    x = [xi + _mm(xi, di) for xi, di in zip(x, d2)]
    d4 = [_mm(d, d) for d in d2]
    x = [xi + _mm(xi, di) for xi, di in zip(x, d4)]
    for lvl in range(1, len(blk_masks)):
        off_mask = blk_masks[lvl] & jnp.logical_not(blk_masks[lvl - 1])
        off = [jnp.where(off_mask, a, 0.0) for a in a_list]
        y = [_mm(xi, oi) for xi, oi in zip(x, off)]
        x = [xi - _mm(yi, xi) for xi, yi in zip(x, y)]
    return x


def _seq_kernel(p_ref, ab_ref, s0_ref, g0_ref, c0_ref, cw_ref, alog_ref, dtb_ref, gnw_ref,
                dww_ref, dwb_ref, lnw_ref, lnb_ref,
                gdn_ref, conf_ref, s_out_ref, g_out_ref, c_out_ref,
                s_scr, xbuf, cbuf, shbuf, *, t, c):
    step = pl.program_id(1)
    last = pl.num_programs(1) - 1
    nch = t // c

    @pl.when(step == 0)
    def _():
        s_scr[...] = s0_ref[...]
        xbuf[0:GHIST_ROWS, :] = g0_ref[...]
        cbuf[0:CHIST_ROWS, :] = c0_ref[...]

    xbuf[GHIST_ROWS:GHIST_ROWS + t, :] = p_ref[0, :, C_QKV:C_QKV + QKV_WIDTH]
    base = GHIST_ROWS - (SHORT_CONV - 1)

    def conv_block(lo):
        sl = slice(lo, lo + LANES)
        acc = cw_ref[0:1, sl] * xbuf[base:base + t, sl]
        for k in range(1, SHORT_CONV):
            acc = acc + cw_ref[k:k + 1, sl] * xbuf[base + k:base + k + t, sl]
        return _silu(acc)

    qn, kn, vv = {}, {}, {}
    for h in range(HEADS):
        qf = conv_block(h * HEAD_K)
        kf = conv_block(GDN_WIDTH + h * HEAD_K)
        vf = conv_block(2 * GDN_WIDTH + h * HEAD_V)
        qf = qf * lax.rsqrt(jnp.sum(qf * qf, axis=-1, keepdims=True) + RMS_EPS) * (HEAD_K ** -0.5)
        kf = kf * lax.rsqrt(jnp.sum(kf * kf, axis=-1, keepdims=True) + RMS_EPS)
        for ci in range(nch):
            rows = slice(ci * c, (ci + 1) * c)
            qn[ci, h], kn[ci, h], vv[ci, h] = qf[rows], kf[rows], vf[rows]
    new_hist = xbuf[t:t + GHIST_ROWS, :]
    xbuf[0:GHIST_ROWS, :] = new_hist

    ab = ab_ref[0]
    g_all = -jnp.exp(alog_ref[...]) * _softplus(ab + dtb_ref[...])
    beta_all = jax.nn.sigmoid(ab)

    r = lax.broadcasted_iota(jnp.int32, (c, c), 0)
    q = lax.broadcasted_iota(jnp.int32, (c, c), 1)
    incl = r >= q
    strict = r > q
    eye = (r == q).astype(F32)
    tril = incl.astype(BF16)
    blk_masks = []
    b = 8
    while b <= c:
        sh = b.bit_length() - 1
        blk_masks.append((r >> sh) == (q >> sh))
        b *= 2

    gcum, gcum_t, beta = [], [], []
    for ci in range(nch):
        g = g_all[ci * c:(ci + 1) * c]
        g_hi = g.astype(BF16)
        r1 = g - g_hi.astype(F32)
        g_mid = r1.astype(BF16)
        g_lo = (r1 - g_mid.astype(F32)).astype(BF16)
        gc = (jnp.dot(tril, g_hi, preferred_element_type=F32)
              + (jnp.dot(tril, g_mid, preferred_element_type=F32)
                 + jnp.dot(tril, g_lo, preferred_element_type=F32)))
        gcum.append(gc)
        gcum_t.append(gc.T)
        beta.append(beta_all[ci * c:(ci + 1) * c])

    keys = [(ci, h) for ci in range(nch) for h in range(HEADS)]
    gc_col = {k: gcum[k[0]][:, k[1]:k[1] + 1] for k in keys}
    beta_col = {k: beta[k[0]][:, HEADS + k[1]:HEADS + k[1] + 1] for k in keys}
    gam = {k: jnp.where(incl, jnp.exp(jnp.where(incl, gc_col[k] - gcum_t[k[0]][k[1]:k[1] + 1, :], 0.0)), 0.0)
           for k in keys}
    kq = {k: _mm_nt(jnp.concatenate([kn[k], qn[k]], axis=0), kn[k]) for k in keys}
    a = {k: jnp.where(strict, beta_col[k] * kq[k][:c] * gam[k], 0.0) for k in keys}
    qkg = {k: kq[k][c:] * gam[k] for k in keys}
    tinv = dict(zip(keys, _tri_inverse_many([a[k] for k in keys], eye, blk_masks)))
    eg = {k: jnp.exp(gc_col[k]) for k in keys}
    sol = {k: _mm(tinv[k], jnp.concatenate([beta_col[k] * vv[k], (beta_col[k] * eg[k]) * kn[k]], axis=-1))
           for k in keys}
    g_last = {k: gc_col[k][c - 1:c, :] for k in keys}
    kd = {k: kn[k] * jnp.exp(g_last[k] - gc_col[k]) for k in keys}

    gnw = gnw_ref[...]
    s = [s_scr[h] for h in range(HEADS)]
    for ci in range(nch):
        rows = slice(ci * c, (ci + 1) * c)
        ks = [(ci, h) for h in range(HEADS)]
        ws = [_mm(jnp.concatenate([sol[k][:, HEAD_V:], qn[k]], axis=0), s[k[1]]) for k in ks]
        v_new = [sol[k][:, :HEAD_V] - w[:c] for k, w in zip(ks, ws)]
        o = [eg[k] * w[c:] + _mm(qkg[k], vn) for k, w, vn in zip(ks, ws, v_new)]
        s = [jnp.exp(g_last[k]) * s[k[1]] + _mm_tn(kd[k], vn) for k, vn in zip(ks, v_new)]
        for h in range(HEADS):
            on = o[h] * lax.rsqrt(jnp.mean(o[h] * o[h], axis=-1, keepdims=True) + RMS_EPS) * gnw
            z = p_ref[0, rows, C_ZG + h * HEAD_V:C_ZG + (h + 1) * HEAD_V]
            gdn_ref[0, rows, h * HEAD_V:(h + 1) * HEAD_V] = (on * _silu(z)).astype(gdn_ref.dtype)
    for h in range(HEADS):
        s_scr[h] = s[h]

    glu = p_ref[0, :, C_GA:C_GA + CONF_WIDTH] * jax.nn.sigmoid(p_ref[0, :, C_GB:C_GB + CONF_WIDTH])
    cbuf[CHIST_ROWS:CHIST_ROWS + t, :] = glu
    cbase = CHIST_ROWS - (CONF_KERNEL - 1)
    span = t + CHIST_ROWS - SUBLANES
    for sft in range(1, SUBLANES):
        shbuf[sft - 1] = cbuf[sft:sft + span, :]
    cv_blocks = []
    for cb in range(CONF_WIDTH // LANES):
        sl = slice(cb * LANES, (cb + 1) * LANES)
        acc = None
        for k in range(CONF_KERNEL):
            m, sft = divmod(cbase + k, SUBLANES)
            if sft == 0:
                xs = cbuf[SUBLANES * m:SUBLANES * m + t, sl]
            else:
                xs = shbuf[sft - 1, SUBLANES * m:SUBLANES * m + t, sl]
            term = dww_ref[k:k + 1, sl] * xs
            acc = term if acc is None else acc + term
        cv_blocks.append(acc + dwb_ref[:, sl])
    cv = jnp.concatenate(cv_blocks, axis=-1)
    mu = jnp.mean(cv, axis=-1, keepdims=True)
    cc = cv - mu
    cn = cc * lax.rsqrt(jnp.mean(cc * cc, axis=-1, keepdims=True) + LN_EPS)
    cn = cn * lnw_ref[...] + lnb_ref[...]
    zc = p_ref[0, :, C_ZC:C_ZC + CONF_WIDTH]
    conf_ref[0] = (_silu(cn) * _silu(zc)).astype(conf_ref.dtype)
    new_chist = cbuf[t:t + CHIST_ROWS, :]
    cbuf[0:CHIST_ROWS, :] = new_chist

    @pl.when(step == last)
    def _():
        s_out_ref[0] = s_scr[...]
        g_out_ref[0] = xbuf[0:GHIST_ROWS, :]
        c_out_ref[0] = cbuf[0:CHIST_ROWS, :]


def _seq_call(p, ab, s0, g0, c0, wts, t, c):
    bsz, seq, _ = p.shape
    cw, alog, dtb, gnw, dww, dwb, lnw, lnb = wts
    full = lambda shape: pl.BlockSpec(shape, lambda b, s: (0,) * len(shape))
    return pl.pallas_call(
        functools.partial(_seq_kernel, t=t, c=c),
        grid=(bsz, seq // t),
        in_specs=[pl.BlockSpec((1, t, P_COLS), lambda b, s: (b, s, 0)),
                  pl.BlockSpec((1, t, LANES), lambda b, s: (b, s, 0)),
                  full((HEADS, HEAD_K, HEAD_V)),
                  full((GHIST_ROWS, QKV_WIDTH)),
                  full((CHIST_ROWS, CONF_WIDTH)),
                  full(cw.shape), full(alog.shape), full(dtb.shape), full(gnw.shape),
                  full(dww.shape), full(dwb.shape), full(lnw.shape), full(lnb.shape)],
        out_specs=[pl.BlockSpec((1, t, GDN_WIDTH), lambda b, s: (b, s, 0)),
                   pl.BlockSpec((1, t, CONF_WIDTH), lambda b, s: (b, s, 0)),
                   pl.BlockSpec((1, HEADS, HEAD_K, HEAD_V), lambda b, s: (b, 0, 0, 0)),
                   pl.BlockSpec((1, GHIST_ROWS, QKV_WIDTH), lambda b, s: (b, 0, 0)),
                   pl.BlockSpec((1, CHIST_ROWS, CONF_WIDTH), lambda b, s: (b, 0, 0))],
        out_shape=[jax.ShapeDtypeStruct((bsz, seq, GDN_WIDTH), BF16),
                   jax.ShapeDtypeStruct((bsz, seq, CONF_WIDTH), BF16),
                   jax.ShapeDtypeStruct((bsz, HEADS, HEAD_K, HEAD_V), F32),
                   jax.ShapeDtypeStruct((bsz, GHIST_ROWS, QKV_WIDTH), F32),
                   jax.ShapeDtypeStruct((bsz, CHIST_ROWS, CONF_WIDTH), F32)],
        scratch_shapes=[pltpu.VMEM((HEADS, HEAD_K, HEAD_V), F32),
                        pltpu.VMEM((t + GHIST_ROWS, QKV_WIDTH), F32),
                        pltpu.VMEM((t + CHIST_ROWS, CONF_WIDTH), F32),
                        pltpu.VMEM((SUBLANES - 1, t + CHIST_ROWS - SUBLANES, CONF_WIDTH), F32)],
        compiler_params=pltpu.CompilerParams(dimension_semantics=("parallel", "arbitrary"),
                                             vmem_limit_bytes=VMEM_LIMIT),
        name=f"seq_t{t}",
    )(p, ab, s0, g0, c0, cw, alog, dtb, gnw, dww, dwb, lnw, lnb)


def _step_kernel(p_ref, ab_ref, s_ref, gh_ref, ch_ref, cw_ref, alog_ref, dtb_ref, gnw_ref,
                 dww_ref, dwb_ref, lnw_ref, lnb_ref,
                 gdn_ref, conf_ref, s_out_ref, g_out_ref, c_out_ref, o_scr, *, nb):
    nh = SHORT_CONV - 1
    nc = CONF_KERNEL - 1
    x = p_ref[:, C_QKV:C_QKV + QKV_WIDTH]
    conv = cw_ref[0:1, :] * gh_ref[0]
    for k in range(1, nh):
        conv = conv + cw_ref[k:k + 1, :] * gh_ref[k]
        g_out_ref[k - 1] = gh_ref[k]
    conv = conv + cw_ref[nh:nh + 1, :] * x
    g_out_ref[nh - 1] = x
    qkv = _silu(conv)

    glu = p_ref[:, C_GA:C_GA + CONF_WIDTH] * jax.nn.sigmoid(p_ref[:, C_GB:C_GB + CONF_WIDTH])
    cv = dww_ref[0:1, :] * ch_ref[0]
    for k in range(1, nc):
        cv = cv + dww_ref[k:k + 1, :] * ch_ref[k]
        c_out_ref[k - 1] = ch_ref[k]
    cv = cv + dww_ref[nc:nc + 1, :] * glu + dwb_ref[...]
    c_out_ref[nc - 1] = glu

    ab = ab_ref[...]
    g_all = -jnp.exp(alog_ref[...]) * _softplus(ab + dtb_ref[...])
    eg_all = jnp.exp(g_all)
    beta_all = jax.nn.sigmoid(ab)
    row = lax.broadcasted_iota(jnp.int32, (nb, HEAD_K), 0)

    for h in range(HEADS):
        qh = qkv[:, h * HEAD_K:(h + 1) * HEAD_K]
        kh = qkv[:, GDN_WIDTH + h * HEAD_K:GDN_WIDTH + (h + 1) * HEAD_K]
        vh = qkv[:, 2 * GDN_WIDTH + h * HEAD_V:2 * GDN_WIDTH + (h + 1) * HEAD_V]
        qn = qh * lax.rsqrt(jnp.sum(qh * qh, axis=-1, keepdims=True) + RMS_EPS) * (HEAD_K ** -0.5)
        kn = kh * lax.rsqrt(jnp.sum(kh * kh, axis=-1, keepdims=True) + RMS_EPS)
        qk = jnp.sum(qn * kn, axis=-1, keepdims=True)
        kq = jnp.concatenate([kn, qn], axis=0).astype(BF16)
        prod = [jnp.dot(kq, s_ref[b, h].astype(BF16), preferred_element_type=F32) for b in range(nb)]
        ks = prod[0][:nb]
        qs = prod[0][nb:]
        for b in range(1, nb):
            ks = jnp.where(row == b, prod[b][:nb], ks)
            qs = jnp.where(row == b, prod[b][nb:], qs)
        eg = eg_all[:, h:h + 1]
        bt = beta_all[:, HEADS + h:HEADS + h + 1]
        v_new = bt * vh - (bt * eg) * ks
        o_scr[:, h * HEAD_V:(h + 1) * HEAD_V] = eg * qs + qk * v_new
        for b in range(nb):
            outer = _mm_tn(jnp.where(row == b, kn, 0.0), v_new)
            s_out_ref[b, h] = eg_all[b:b + 1, h:h + 1] * s_ref[b, h] + outer

    gnw = gnw_ref[...]
    for h in range(HEADS):
        o = o_scr[:, h * HEAD_V:(h + 1) * HEAD_V]
        on = o * lax.rsqrt(jnp.mean(o * o, axis=-1, keepdims=True) + RMS_EPS) * gnw
        z = p_ref[:, C_ZG + h * HEAD_V:C_ZG + (h + 1) * HEAD_V]
        gdn_ref[:, h * HEAD_V:(h + 1) * HEAD_V] = (on * _silu(z)).astype(gdn_ref.dtype)

    mu = jnp.mean(cv, axis=-1, keepdims=True)
    cc = cv - mu
    cn = cc * lax.rsqrt(jnp.mean(cc * cc, axis=-1, keepdims=True) + LN_EPS)
    cn = cn * lnw_ref[...] + lnb_ref[...]
    conf_ref[...] = (_silu(cn) * _silu(p_ref[:, C_ZC:C_ZC + CONF_WIDTH])).astype(conf_ref.dtype)


def _step_call(p, ab, s, gh, ch, wts, nb, out_dtype):
    bsz = p.shape[0]
    cw, alog, dtb, gnw, dww, dwb, lnw, lnb = wts
    full = lambda shape: pl.BlockSpec(shape, lambda i: (0,) * len(shape))
    nh = SHORT_CONV - 1
    nc = CONF_KERNEL - 1
    return pl.pallas_call(
        functools.partial(_step_kernel, nb=nb),
        grid=(bsz // nb,),
        in_specs=[pl.BlockSpec((nb, P_COLS), lambda i: (i, 0)),
                  pl.BlockSpec((nb, LANES), lambda i: (i, 0)),
                  pl.BlockSpec((nb, HEADS, HEAD_K, HEAD_V), lambda i: (i, 0, 0, 0)),
                  pl.BlockSpec((nh, nb, QKV_WIDTH), lambda i: (0, i, 0)),
                  pl.BlockSpec((nc, nb, CONF_WIDTH), lambda i: (0, i, 0)),
                  full(cw.shape), full(alog.shape), full(dtb.shape), full(gnw.shape),
                  full(dww.shape), full(dwb.shape), full(lnw.shape), full(lnb.shape)],
        out_specs=[pl.BlockSpec((nb, GDN_WIDTH), lambda i: (i, 0)),
                   pl.BlockSpec((nb, CONF_WIDTH), lambda i: (i, 0)),
                   pl.BlockSpec((nb, HEADS, HEAD_K, HEAD_V), lambda i: (i, 0, 0, 0)),
                   pl.BlockSpec((nh, nb, QKV_WIDTH), lambda i: (0, i, 0)),
                   pl.BlockSpec((nc, nb, CONF_WIDTH), lambda i: (0, i, 0))],
        out_shape=[jax.ShapeDtypeStruct((bsz, GDN_WIDTH), BF16),
                   jax.ShapeDtypeStruct((bsz, CONF_WIDTH), BF16),
                   jax.ShapeDtypeStruct(s.shape, out_dtype),
                   jax.ShapeDtypeStruct(gh.shape, out_dtype),
                   jax.ShapeDtypeStruct(ch.shape, out_dtype)],
        scratch_shapes=[pltpu.VMEM((nb, GDN_WIDTH), F32)],
        compiler_params=pltpu.CompilerParams(dimension_semantics=("parallel",),
                                             vmem_limit_bytes=VMEM_LIMIT),
        name="decode_step",
    )(p, ab, s, gh, ch, cw, alog, dtb, gnw, dww, dwb, lnw, lnb)


def _out_kernel(gdn_ref, conf_ref, w_ref, x_ref, fw_ref, o_ref):
    y = (jnp.dot(gdn_ref[...], w_ref[0:GDN_WIDTH, :], preferred_element_type=F32)
         + jnp.dot(conf_ref[...], w_ref[GDN_WIDTH:, :], preferred_element_type=F32))
    hres = x_ref[...] + y
    o_ref[...] = hres * lax.rsqrt(jnp.mean(hres * hres, axis=-1, keepdims=True) + RMS_EPS) * fw_ref[...]


def _out_call(gdn, conf, w, x, fw, tm):
    m = x.shape[0]
    return pl.pallas_call(
        _out_kernel,
        grid=(m // tm,),
        in_specs=[pl.BlockSpec((tm, GDN_WIDTH), lambda i: (i, 0)),
                  pl.BlockSpec((tm, CONF_WIDTH), lambda i: (i, 0)),
                  pl.BlockSpec(w.shape, lambda i: (0, 0)),
                  pl.BlockSpec((tm, D_MODEL), lambda i: (i, 0)),
                  pl.BlockSpec((1, D_MODEL), lambda i: (0, 0))],
        out_specs=pl.BlockSpec((tm, D_MODEL), lambda i: (i, 0)),
        out_shape=jax.ShapeDtypeStruct((m, D_MODEL), F32),
        compiler_params=pltpu.CompilerParams(dimension_semantics=("parallel",),
                                             vmem_limit_bytes=VMEM_LIMIT),
        name="out_proj",
    )(gdn, conf, w, x, fw)


def _pad_lanes(v):
    return jnp.pad(v.astype(F32), (0, LANES - v.shape[0]))[None, :]


def kernel(x_prompt, x_sample, state_gdn_S, state_gdn_conv, state_conf_conv, meta_tokens, norm_w, w_in,
           gdn_conv_w, gdn_A_log, gdn_dt_bias, gdn_norm_w, conf_dw_w, conf_dw_b, conf_ln_w, conf_ln_b,
           w_out, final_norm_w):
    bsz, seq, _ = x_prompt.shape
    dec = x_sample.shape[0]
    l = 0
    wt = jnp.swapaxes(w_in[l], 0, 1)
    w_o = w_out[l].astype(BF16)
    nw = norm_w[l][None, :]
    fw = final_norm_w[None, :]
    wts = (gdn_conv_w[l], _pad_lanes(gdn_A_log[l]), _pad_lanes(gdn_dt_bias[l]), gdn_norm_w[l][None, :],
           conf_dw_w[l], conf_dw_b[l][None, :], conf_ln_w[l][None, :], conf_ln_b[l][None, :])

    xp = x_prompt.reshape(bsz * seq, D_MODEL)
    xs = x_sample.reshape(dec, D_MODEL)
    x_small = jnp.concatenate([xs, meta_tokens.astype(xs.dtype)], axis=0)
    n_small = dec + N_META

    xn_p, ab_prompt = _norm_cast(xp, nw, wt, 512)
    xn_s, ab_small = _norm_cast(x_small, nw, wt, n_small)
    n_main = P_COLS // IN_TILE
    p_prompt, p_small = _in_proj(xn_p, xn_s, wt, _main_row_start, n_main, IN_TILE, 1024, "in_proj")

    zs = jnp.zeros((HEADS, HEAD_K, HEAD_V), F32)
    zg = jnp.zeros((GHIST_ROWS, QKV_WIDTH), F32)
    zc = jnp.zeros((CHIST_ROWS, CONF_WIDTH), F32)
    _, _, s_m, g_m, c_m = _seq_call(p_small[dec:][None], ab_small[dec:][None], zs, zg, zc, wts, N_META, N_META)

    gdn_p, conf_p, s_p, g_p, c_p = _seq_call(p_prompt.reshape(bsz, seq, P_COLS),
                                             ab_prompt.reshape(bsz, seq, LANES),
                                             s_m[0], g_m[0], c_m[0], wts, 4 * GDN_CHUNK, GDN_CHUNK)
    y_prompt = _out_call(gdn_p.reshape(bsz * seq, GDN_WIDTH), conf_p.reshape(bsz * seq, CONF_WIDTH),
                         w_o, xp, fw, 512).reshape(bsz, seq, D_MODEL)

    sdt = state_gdn_S.dtype
    gh = jnp.swapaxes(state_gdn_conv[l], 0, 1)
    ch = jnp.swapaxes(state_conf_conv[l], 0, 1)
    gdn_s, conf_s, s_s, g_s, c_s = _step_call(p_small[:dec], ab_small[:dec], state_gdn_S[l], gh, ch,
                                              wts, 2 * SUBLANES, sdt)
    y_sample = _out_call(gdn_s, conf_s, w_o, xs, fw, dec).reshape(dec, 1, D_MODEL)

    nh = SHORT_CONV - 1
    nc = CONF_KERNEL - 1
    return (y_prompt, y_sample,
            s_p.astype(sdt)[None],
            g_p[:, GHIST_ROWS - nh:, :].astype(state_gdn_conv.dtype)[None],
            c_p[:, CHIST_ROWS - nc:, :].astype(state_conf_conv.dtype)[None],
            s_s[None], jnp.swapaxes(g_s, 0, 1)[None], jnp.swapaxes(c_s, 0, 1)[None])
```
